```python
import jax, jax.numpy as jnp
from jax import lax
import numpy as np

D_MODEL = 1024
BATCH = 4
SEQ = 8192
DEPTH = 1

N_META = 16
BLOCK = 128
PAD = BLOCK - N_META
HEAD_DIM = 64
SB_HEADS = 8
SB_WIDTH = SB_HEADS * HEAD_DIM
SWA_Q_HEADS = 16
SWA_KV_HEADS = 2
SWA_GROUP = SWA_Q_HEADS // SWA_KV_HEADS
SWA_WIDTH = SWA_Q_HEADS * HEAD_DIM
SWA_KV_WIDTH = SWA_KV_HEADS * HEAD_DIM
WINDOW = 128
ROPE_THETA = 10000.0
RMS_EPS = 1e-6
SPLITS = (SB_WIDTH, SB_WIDTH, SB_WIDTH, SWA_WIDTH, SWA_KV_WIDTH, SWA_KV_WIDTH, SB_WIDTH, SWA_WIDTH, D_MODEL, D_MODEL)
IN_COLS = sum(SPLITS)

kernel_name = "hybrid_stickbreak_swa_sink_gated"


def rms_norm(x, g):
    xf = x.astype(jnp.float32)
    y = xf * lax.rsqrt(jnp.mean(xf * xf, axis=-1, keepdims=True) + RMS_EPS)
    return (y * g.astype(jnp.float32)).astype(x.dtype)


def rope(x, pos):
    half = HEAD_DIM // 2
    inv = ROPE_THETA ** (-jnp.arange(half, dtype=jnp.float32) / half)
    ang = pos.astype(jnp.float32)[:, None] * inv[None, :]
    cos = jnp.cos(ang)[None, :, None, :]
    sin = jnp.sin(ang)[None, :, None, :]
    x1 = x[..., :half].astype(jnp.float32)
    x2 = x[..., half:].astype(jnp.float32)
    out = jnp.concatenate([x1 * cos - x2 * sin, x2 * cos + x1 * sin], axis=-1)
    return out.astype(x.dtype)


def stick_breaking_attention(q, k, v, valid):
    B, L, H, d = q.shape
    scale = d ** -0.5
    outs = []
    for blk in range(L // BLOCK):
        q0 = blk * BLOCK
        end = q0 + BLOCK
        z = jnp.einsum('bqhd,bkhd->bhqk', q[:, q0:end], k[:, :end],
                       preferred_element_type=jnp.float32) * scale
        t = q0 + jnp.arange(BLOCK)
        s = jnp.arange(end)
        mask = (s[None, :] < t[:, None]) & valid[None, :end]
        log_beta = jax.nn.log_sigmoid(z)
        log_1m = jnp.where(mask, log_beta - z, 0.0)
        rev = lax.cumsum(log_1m, axis=3, reverse=True)
        suffix = jnp.concatenate([rev[..., 1:], jnp.zeros_like(rev[..., :1])], axis=-1)
        w = jnp.where(mask, jnp.exp(log_beta + suffix), 0.0)
        outs.append(jnp.einsum('bhqk,bkhd->bqhd', w.astype(v.dtype), v[:, :end]))
    return jnp.concatenate(outs, axis=1)


def sliding_window_sink_attention(q, k, v, sinks, valid):
    B, L, Hq, d = q.shape
    nb = L // BLOCK
    scale = d ** -0.5
    qb = q.reshape(B, nb, BLOCK, SWA_KV_HEADS, SWA_GROUP, d)

    def band(t):
        tb = t.reshape(B, nb, BLOCK, SWA_KV_HEADS, d)
        prev = jnp.pad(tb[:, :-1], ((0, 0), (1, 0), (0, 0), (0, 0), (0, 0)))
        return jnp.concatenate([prev, tb], axis=2)

    kb, vb = band(k), band(v)
    vblk = valid.reshape(nb, BLOCK)
    kvalid = jnp.concatenate([jnp.pad(vblk[:-1], ((1, 0), (0, 0)), constant_values=False), vblk], axis=1)
    scores = jnp.einsum('bnqhgd,bnkhd->bnhgqk', qb, kb,
                        preferred_element_type=jnp.float32) * scale
    diff = (BLOCK + jnp.arange(BLOCK))[:, None] - jnp.arange(2 * BLOCK)[None, :]
    mask = ((diff >= 0) & (diff < WINDOW))[None] & kvalid[:, None, :]
    scores = jnp.where(mask[None, :, None, None], scores, -jnp.inf)
    sink = sinks.astype(jnp.float32).reshape(SWA_KV_HEADS, SWA_GROUP)[None, None, :, :, None, None]
    sink = jnp.broadcast_to(sink, scores.shape[:-1] + (1,))
    probs = jax.nn.softmax(jnp.concatenate([scores, sink], axis=-1), axis=-1)[..., :-1]
    o = jnp.einsum('bnhgqk,bnkhd->bnqhgd', probs.astype(v.dtype), vb)
    return o.reshape(B, L, Hq * d)


def setup_inputs(seed: int = 0) -> dict:
    key = jax.random.key(seed)
    ks = jax.random.split(key, 10)
    f32 = jnp.float32
    x = jax.random.normal(ks[0], (BATCH, SEQ, D_MODEL), f32)
    meta_tokens = jax.random.normal(ks[1], (N_META, D_MODEL), f32)
    norm_gain = 1.0 + 0.02 * jax.random.normal(ks[2], (DEPTH, D_MODEL), f32)
    w_in = jax.random.normal(ks[3], (DEPTH, D_MODEL, IN_COLS), f32) * D_MODEL ** -0.5
    w_branch_sb = jax.random.normal(ks[4], (DEPTH, SB_WIDTH, D_MODEL), f32) * SB_WIDTH ** -0.5
    w_branch_swa = jax.random.normal(ks[5], (DEPTH, SWA_WIDTH, D_MODEL), f32) * SWA_WIDTH ** -0.5
    w_out = jax.random.normal(ks[6], (DEPTH, D_MODEL, D_MODEL), f32) * D_MODEL ** -0.5
    attn_sinks = jax.random.normal(ks[7], (DEPTH, SWA_Q_HEADS), f32)
    final_norm_gain = 1.0 + 0.02 * jax.random.normal(ks[8], (D_MODEL,), f32)
    return {"x": x, "meta_tokens": meta_tokens, "norm_gain": norm_gain, "w_in": w_in,
            "w_branch_sb": w_branch_sb, "w_branch_swa": w_branch_swa, "w_out": w_out,
            "attn_sinks": attn_sinks, "final_norm_gain": final_norm_gain}


def reference(x, meta_tokens, norm_gain, w_in, w_branch_sb, w_branch_swa, w_out, attn_sinks, final_norm_gain):
    B = x.shape[0]
    meta = jnp.broadcast_to(meta_tokens[None].astype(x.dtype), (B, N_META, D_MODEL))
    pad = jnp.zeros((B, PAD, D_MODEL), x.dtype)
    h = jnp.concatenate([pad, meta, x], axis=1)
    L = h.shape[1]
    idx = jnp.arange(L)
    valid = idx >= PAD
    pos = idx - PAD
    offsets = [int(o) for o in np.cumsum(SPLITS)[:-1]]
    for l in range(DEPTH):
        xn = rms_norm(h, norm_gain[l])
        proj = xn @ w_in[l]
        sb_q, sb_k, sb_v, sw_q, sw_k, sw_v, sb_z, sw_z, g_sb, g_sw = jnp.split(proj, offsets, axis=-1)
        o_sb = stick_breaking_attention(sb_q.reshape(B, L, SB_HEADS, HEAD_DIM),
                                        sb_k.reshape(B, L, SB_HEADS, HEAD_DIM),
                                        sb_v.reshape(B, L, SB_HEADS, HEAD_DIM), valid).reshape(B, L, SB_WIDTH)
        q = rope(sw_q.reshape(B, L, SWA_Q_HEADS, HEAD_DIM), pos)
        k = rope(sw_k.reshape(B, L, SWA_KV_HEADS, HEAD_DIM), pos)
        o_sw = sliding_window_sink_attention(q, k, sw_v.reshape(B, L, SWA_KV_HEADS, HEAD_DIM),
                                             attn_sinks[l], valid)
        y_sb = (o_sb * jax.nn.silu(sb_z)) @ w_branch_sb[l]
        y_sw = (o_sw * jax.nn.silu(sw_z)) @ w_branch_swa[l]
        merged = jax.nn.sigmoid(g_sb) * y_sb + jax.nn.sigmoid(g_sw) * y_sw
        h = h + merged @ w_out[l]
    return rms_norm(h, final_norm_gain)[:, BLOCK:]
```

```python
import functools

import jax
import jax.numpy as jnp
from jax import lax
from jax.experimental import pallas as pl
from jax.experimental.pallas import tpu as pltpu

F32 = jnp.float32
BF16 = jnp.bfloat16

D_MODEL = 1024
N_META = 16
BLOCK = 128
PAD = BLOCK - N_META
HEAD_DIM = 64
SB_HEADS = 8
SB_WIDTH = SB_HEADS * HEAD_DIM
SWA_Q_HEADS = 16
SWA_KV_HEADS = 2
SWA_GROUP = SWA_Q_HEADS // SWA_KV_HEADS
SWA_WIDTH = SWA_Q_HEADS * HEAD_DIM
SWA_KV_WIDTH = SWA_KV_HEADS * HEAD_DIM
ROPE_THETA = 10000.0
RMS_EPS = 1e-6
SCALE = HEAD_DIM ** -0.5

LANES = 128
PAIR = LANES // HEAD_DIM
SB_PAIRS = SB_HEADS // PAIR
SWA_PAIRS = SWA_Q_HEADS // PAIR
KV_DUP_WIDTH = SWA_KV_HEADS * LANES

F32_EXP_UNDERFLOW = -104.0
MASKED_SCORE = -1e30

_C_SBQ = 0
_C_SBK = _C_SBQ + SB_WIDTH
_C_SBV = _C_SBK + SB_WIDTH
_C_SWQ = _C_SBV + SB_WIDTH
_C_SWK = _C_SWQ + SWA_WIDTH
_C_SWV = _C_SWK + KV_DUP_WIDTH
_C_ZSB = _C_SWV + KV_DUP_WIDTH
_C_ZSW = _C_ZSB + SB_WIDTH
_C_GSB = _C_ZSW + SWA_WIDTH
_C_GSW = _C_GSB + D_MODEL
_C_END = _C_GSW + D_MODEL

PROJ_TILE_ROWS = 512
OUT_TILE_ROWS = 512
VMEM_LIMIT_BYTES = 56 * 1024 * 1024


def _sigmoid(z):
    return 1.0 / (1.0 + jnp.exp(-z))


def _rope_tile(x, cos, sin_signed, first_half):
    partner = jnp.where(first_half, pltpu.roll(x, LANES - HEAD_DIM // 2, 1), pltpu.roll(x, HEAD_DIM // 2, 1))
    return x * cos + partner * sin_signed


def _proj_kernel(x_ref, gain_ref, w_ref, cos_ref, sin_ref,
                 sbq_ref, sbk_ref, sbv_ref, swq_ref, swk_ref, swv_ref,
                 asb_ref, asw_ref, ssb_ref, ssw_ref):
    x = x_ref[...]
    ms = jnp.mean(x * x, axis=-1, keepdims=True)
    xn = (x * lax.rsqrt(ms + RMS_EPS) * gain_ref[...]).astype(BF16)

    def mm(c0, width):
        return jnp.dot(xn, w_ref[:, c0:c0 + width], preferred_element_type=F32)

    cos = cos_ref[...]
    sin_signed = sin_ref[...]
    lane = lax.broadcasted_iota(jnp.int32, cos.shape, 1)
    first_half = (lane % HEAD_DIM) < (HEAD_DIM // 2)

    def roped(acc, scale):
        tiles = []
        for t in range(acc.shape[1] // LANES):
            r = _rope_tile(acc[:, t * LANES:(t + 1) * LANES], cos, sin_signed, first_half)
            tiles.append(r * scale if scale != 1.0 else r)
        return jnp.concatenate(tiles, axis=1)

    sbq_ref[...] = (mm(_C_SBQ, SB_WIDTH) * SCALE).astype(BF16)
    sbk_ref[...] = mm(_C_SBK, SB_WIDTH).astype(BF16)
    sbv_ref[...] = mm(_C_SBV, SB_WIDTH).astype(BF16)
    half = SWA_WIDTH // 2
    for c in range(2):
        swq_ref[:, c * half:(c + 1) * half] = roped(mm(_C_SWQ + c * half, half), SCALE).astype(BF16)
    swk_ref[...] = roped(mm(_C_SWK, KV_DUP_WIDTH), 1.0).astype(BF16)
    swv_ref[...] = mm(_C_SWV, KV_DUP_WIDTH).astype(BF16)
    z = mm(_C_ZSB, SB_WIDTH)
    asb_ref[...] = (z * _sigmoid(z)).astype(BF16)
    for c in range(2):
        z = mm(_C_ZSW + c * half, half)
        asw_ref[:, c * half:(c + 1) * half] = (z * _sigmoid(z)).astype(BF16)
    for c in range(2):
        ssb_ref[:, c * half:(c + 1) * half] = _sigmoid(mm(_C_GSB + c * half, half)).astype(BF16)
    for c in range(2):
        ssw_ref[:, c * half:(c + 1) * half] = _sigmoid(mm(_C_GSW + c * half, half)).astype(BF16)


def _proj_call(x2d, gain, w, cos, sin_signed, tile_rows):
    n = x2d.shape[0]
    table_tiles = cos.shape[0] // tile_rows
    widths = (SB_WIDTH, SB_WIDTH, SB_WIDTH, SWA_WIDTH, KV_DUP_WIDTH, KV_DUP_WIDTH,
              SB_WIDTH, SWA_WIDTH, D_MODEL, D_MODEL)
    row_tile = lambda width: pl.BlockSpec((tile_rows, width), lambda i: (i, 0))
    table = pl.BlockSpec((tile_rows, LANES), lambda i: (i % table_tiles, 0))
    return pl.pallas_call(
        _proj_kernel,
        grid=(n // tile_rows,),
        in_specs=[row_tile(D_MODEL),
                  pl.BlockSpec((1, D_MODEL), lambda i: (0, 0)),
                  pl.BlockSpec((D_MODEL, _C_END), lambda i: (0, 0), pipeline_mode=pl.Buffered(1)),
                  table, table],
        out_specs=[row_tile(wd) for wd in widths],
        out_shape=[jax.ShapeDtypeStruct((n, wd), BF16) for wd in widths],
        compiler_params=pltpu.CompilerParams(dimension_semantics=("arbitrary",),
                                             vmem_limit_bytes=VMEM_LIMIT_BYTES),
        name="proj",
    )(x2d, gain, w, cos, sin_signed)


def _sb_kernel(q_ref, k_ref, v_ref, km_ref, vm_ref, a_ref, o_ref):
    i = pl.program_id(1)
    row = lax.broadcasted_iota(jnp.int32, (BLOCK, LANES), 0)
    col = lax.broadcasted_iota(jnp.int32, (BLOCK, LANES), 1)
    low_half = col < HEAD_DIM
    later = (row > col).astype(BF16)
    later2 = jnp.concatenate([later, later], axis=0)
    causal = col < row
    meta_valid = col >= PAD

    q = q_ref[...]
    zero = jnp.zeros((BLOCK, LANES), BF16)
    q_heads = []
    for p in range(SB_PAIRS):
        qp = q[:, p * LANES:(p + 1) * LANES]
        q_heads.append(jnp.where(low_half, qp, zero))
        q_heads.append(jnp.where(low_half, zero, qp))

    def key_block(kb, vb, mask, carry, acc):
        new_carry, new_acc = [], []
        for p in range(SB_PAIRS):
            kp = kb[:, p * LANES:(p + 1) * LANES]
            vp = vb[:, p * LANES:(p + 1) * LANES]
            pv = []
            for hh in range(PAIR):
                h = p * PAIR + hh
                z = lax.dot_general(q_heads[h], kp, (((1,), (1,)), ((), ())), preferred_element_type=F32)
                softplus_tail = jnp.log(1.0 + jnp.exp(-jnp.abs(z)))
                log_beta = jnp.minimum(z, 0.0) - softplus_tail
                log_1m = log_beta - z
                if mask is not None:
                    log_1m = jnp.where(mask, log_1m, 0.0)
                hi = log_1m.astype(BF16)
                lo = (log_1m - hi.astype(F32)).astype(BF16)
                suffix = jnp.dot(jnp.concatenate([hi, lo], axis=1), later2, preferred_element_type=F32)
                w = jnp.exp(log_beta + suffix + carry[h])
                if mask is not None:
                    w = jnp.where(mask, w, 0.0)
                pv.append(jnp.dot(w.astype(BF16), vp, preferred_element_type=F32))
                new_carry.append(carry[h] + jnp.sum(log_1m, axis=1, keepdims=True))
            new_acc.append(acc[p] + jnp.where(low_half, pv[0], pv[1]))
        return new_carry, new_acc

    def least_decayed(carry):
        m = carry[0]
        for c in carry[1:]:
            m = jnp.maximum(m, c)
        return jnp.max(m)

    carry0 = [jnp.zeros((BLOCK, 1), F32) for _ in range(SB_HEADS)]
    acc0 = [jnp.zeros((BLOCK, LANES), F32) for _ in range(SB_PAIRS)]
    start = pl.multiple_of(i * BLOCK, BLOCK)
    carry, acc = key_block(k_ref[pl.ds(start, BLOCK), :], v_ref[pl.ds(start, BLOCK), :], causal, carry0, acc0)

    def cond(state):
        j, top, _, _ = state
        return jnp.logical_and(j >= 0, top >= F32_EXP_UNDERFLOW)

    def body(state):
        j, _, carry, acc = state
        s = pl.multiple_of(j * BLOCK, BLOCK)
        carry, acc = key_block(k_ref[pl.ds(s, BLOCK), :], v_ref[pl.ds(s, BLOCK), :], None, carry, acc)
        return j - 1, least_decayed(carry), carry, acc

    _, top, carry, acc = lax.while_loop(cond, body, (i - 1, least_decayed(carry), carry, acc))

    def with_meta(carry, acc):
        return key_block(km_ref[...], vm_ref[...], meta_valid, carry, acc)[1]

    acc = lax.cond(top >= F32_EXP_UNDERFLOW, with_meta, lambda carry, acc: acc, carry, acc)
    for p in range(SB_PAIRS):
        gate = a_ref[:, p * LANES:(p + 1) * LANES].astype(F32)
        o_ref[:, p * LANES:(p + 1) * LANES] = (acc[p] * gate).astype(BF16)


def _sb_call(q, k, v, k_meta, v_meta, act):
    b, seq, width = q.shape
    q_tile = pl.BlockSpec((None, BLOCK, width), lambda bi, i: (bi, i, 0))
    whole = pl.BlockSpec((None, seq, width), lambda bi, i: (bi, 0, 0), pipeline_mode=pl.Buffered(1))
    meta = pl.BlockSpec((BLOCK, width), lambda bi, i: (0, 0))
    return pl.pallas_call(
        _sb_kernel,
        grid=(b, seq // BLOCK),
        in_specs=[q_tile, whole, whole, meta, meta, q_tile],
        out_specs=q_tile,
        out_shape=jax.ShapeDtypeStruct((b, seq, width), BF16),
        compiler_params=pltpu.CompilerParams(dimension_semantics=("arbitrary", "arbitrary"),
                                             vmem_limit_bytes=VMEM_LIMIT_BYTES),
        name="sb_attn",
    )(q, k, v, k_meta, v_meta, act)


def _swa_kernel(sink_ref, q_ref, kp_ref, ko_ref, vp_ref, vo_ref, km_ref, vm_ref, a_ref, o_ref):
    i = pl.program_id(1)
    first = i == 0
    k_cat = jnp.concatenate([jnp.where(first, km_ref[...], kp_ref[...]), ko_ref[...]], axis=0)
    v_cat = jnp.concatenate([jnp.where(first, vm_ref[...], vp_ref[...]), vo_ref[...]], axis=0)
    t = lax.broadcasted_iota(jnp.int32, (BLOCK, 2 * BLOCK), 0)
    c = lax.broadcasted_iota(jnp.int32, (BLOCK, 2 * BLOCK), 1)
    first_valid = jnp.where(first, PAD, 0)
    mask = (c > t) & (c <= t + BLOCK) & (c >= first_valid)
    low_half = lax.broadcasted_iota(jnp.int32, (BLOCK, LANES), 1) < HEAD_DIM
    zero = jnp.zeros((BLOCK, LANES), BF16)
    for p in range(SWA_PAIRS):
        g = (p * PAIR) // SWA_GROUP
        kg = k_cat[:, g * LANES:(g + 1) * LANES]
        vg = v_cat[:, g * LANES:(g + 1) * LANES]
        qp = q_ref[:, p * LANES:(p + 1) * LANES]
        outs = []
        for hh in range(PAIR):
            qm = jnp.where(low_half, qp, zero) if hh == 0 else jnp.where(low_half, zero, qp)
            s = lax.dot_general(qm, kg, (((1,), (1,)), ((), ())), preferred_element_type=F32)
            s = jnp.where(mask, s, MASKED_SCORE)
            sink = sink_ref[p * PAIR + hh]
            m = jnp.maximum(jnp.max(s, axis=1, keepdims=True), sink)
            e = jnp.exp(s - m)
            denom = jnp.sum(e, axis=1, keepdims=True) + jnp.exp(sink - m)
            o = jnp.dot(e.astype(BF16), vg, preferred_element_type=F32)
            outs.append(o * (1.0 / denom))
        gate = a_ref[:, p * LANES:(p + 1) * LANES].astype(F32)
        o_ref[:, p * LANES:(p + 1) * LANES] = (jnp.where(low_half, outs[0], outs[1]) * gate).astype(BF16)


def _swa_call(sinks, q, k, v, k_meta, v_meta, act):
    b, seq, width = q.shape
    kvw = k.shape[2]
    q_tile = pl.BlockSpec((None, BLOCK, width), lambda bi, i: (bi, i, 0))
    kv_own = pl.BlockSpec((None, BLOCK, kvw), lambda bi, i: (bi, i, 0))
    kv_prev = pl.BlockSpec((None, BLOCK, kvw), lambda bi, i: (bi, jnp.maximum(i - 1, 0), 0))
    meta = pl.BlockSpec((BLOCK, kvw), lambda bi, i: (0, 0))
    return pl.pallas_call(
        _swa_kernel,
        grid=(b, seq // BLOCK),
        in_specs=[pl.BlockSpec(memory_space=pltpu.SMEM),
                  q_tile, kv_prev, kv_own, kv_prev, kv_own, meta, meta, q_tile],
        out_specs=q_tile,
        out_shape=jax.ShapeDtypeStruct((b, seq, width), BF16),
        compiler_params=pltpu.CompilerParams(dimension_semantics=("arbitrary", "arbitrary")),
        name="swa_attn",
    )(sinks, q, k, k, v, v, k_meta, v_meta, act)


def _out_kernel(x_ref, asb_ref, asw_ref, ssb_ref, ssw_ref, wsb_ref, wsw_ref, wout_ref, gain_ref, o_ref):
    y_sb = jnp.dot(asb_ref[...], wsb_ref[...], preferred_element_type=F32)
    y_sw = jnp.dot(asw_ref[...], wsw_ref[...], preferred_element_type=F32)
    merged = ssb_ref[...].astype(F32) * y_sb + ssw_ref[...].astype(F32) * y_sw
    h = x_ref[...] + jnp.dot(merged.astype(BF16), wout_ref[...], preferred_element_type=F32)
    ms = jnp.mean(h * h, axis=-1, keepdims=True)
    o_ref[...] = h * lax.rsqrt(ms + RMS_EPS) * gain_ref[...]


def _out_call(x2d, act_sb, act_sw, sig_sb, sig_sw, w_sb, w_sw, w_out, gain, tile_rows):
    n = x2d.shape[0]
    row_tile = lambda width: pl.BlockSpec((tile_rows, width), lambda i: (i, 0))
    const = lambda shape: pl.BlockSpec(shape, lambda i: (0, 0))
    return pl.pallas_call(
        _out_kernel,
        grid=(n // tile_rows,),
        in_specs=[row_tile(D_MODEL), row_tile(SB_WIDTH), row_tile(SWA_WIDTH), row_tile(D_MODEL),
                  row_tile(D_MODEL), const(w_sb.shape), const(w_sw.shape), const(w_out.shape),
                  const((1, D_MODEL))],
        out_specs=row_tile(D_MODEL),
        out_shape=jax.ShapeDtypeStruct((n, D_MODEL), F32),
        compiler_params=pltpu.CompilerParams(dimension_semantics=("arbitrary",),
                                             vmem_limit_bytes=VMEM_LIMIT_BYTES),
        name="out_proj",
    )(x2d, act_sb, act_sw, sig_sb, sig_sw, w_sb, w_sw, w_out, gain)


def _rope_tables(pos):
    half = HEAD_DIM // 2
    inv = ROPE_THETA ** (-jnp.arange(half, dtype=F32) / half)
    ang = pos.astype(F32)[:, None] * inv[None, :]
    cos, sin = jnp.cos(ang), jnp.sin(ang)
    reps = LANES // HEAD_DIM
    return (jnp.tile(jnp.concatenate([cos, cos], axis=1), (1, reps)),
            jnp.tile(jnp.concatenate([-sin, sin], axis=1), (1, reps)))


def _relayout_w_in(w):
    offs = [0]
    for width in (SB_WIDTH, SB_WIDTH, SB_WIDTH, SWA_WIDTH, SWA_KV_WIDTH, SWA_KV_WIDTH,
                  SB_WIDTH, SWA_WIDTH, D_MODEL, D_MODEL):
        offs.append(offs[-1] + width)
    head = lambda base, g: w[:, base + g * HEAD_DIM: base + (g + 1) * HEAD_DIM]
    dup = lambda base: [head(base, g) for g in range(SWA_KV_HEADS) for _ in range(PAIR)]
    parts = [w[:, :offs[4]]] + dup(offs[4]) + dup(offs[5]) + [w[:, offs[6]:]]
    return jnp.concatenate(parts, axis=1).astype(BF16)


def kernel(x, meta_tokens, norm_gain, w_in, w_branch_sb, w_branch_swa, w_out, attn_sinks, final_norm_gain):
    b, seq, d = x.shape
    assert d == D_MODEL and norm_gain.shape[0] == 1 and seq % PROJ_TILE_ROWS == 0
    n = b * seq
    x2d = x.reshape(n, d)
    w = _relayout_w_in(w_in[0])
    assert w.shape[1] == _C_END
    gain = norm_gain[0].reshape(1, d)

    cos, sin_signed = _rope_tables(jnp.arange(seq) + N_META)
    tok = _proj_call(x2d, gain, w, cos, sin_signed, PROJ_TILE_ROWS)
    sbq, sbk, sbv, swq, swk, swv, act_sb, act_sw, sig_sb, sig_sw = tok

    meta_tile = jnp.concatenate([jnp.zeros((PAD, d), x.dtype), meta_tokens.astype(x.dtype)], axis=0)
    mcos, msin = _rope_tables(jnp.maximum(jnp.arange(BLOCK) - PAD, 0))
    meta = _proj_call(meta_tile, gain, w, mcos, msin, BLOCK)
    sbk_m, sbv_m, swk_m, swv_m = meta[1], meta[2], meta[4], meta[5]

    r3 = lambda a: a.reshape(b, seq, a.shape[1])
    gated_sb = _sb_call(r3(sbq), r3(sbk), r3(sbv), sbk_m, sbv_m, r3(act_sb))
    gated_sw = _swa_call(attn_sinks[0].astype(F32), r3(swq), r3(swk), r3(swv), swk_m, swv_m, r3(act_sw))

    out = _out_call(x2d, gated_sb.reshape(n, SB_WIDTH), gated_sw.reshape(n, SWA_WIDTH), sig_sb, sig_sw,
                    w_branch_sb[0].astype(BF16), w_branch_swa[0].astype(BF16), w_out[0].astype(BF16),
                    final_norm_gain.reshape(1, d), OUT_TILE_ROWS)
    return out.reshape(b, seq, d)
```

```python
import jax
import jax.numpy as jnp
from jax import lax
from jax.experimental import pallas as pl
from jax.experimental.pallas import tpu as pltpu

F32 = jnp.float32
BF16 = jnp.bfloat16

D_MODEL = 1024
N_META = 16
BLOCK = 128
PAD = BLOCK - N_META
HEAD_DIM = 64
SB_HEADS = 8
SB_WIDTH = SB_HEADS * HEAD_DIM
SWA_Q_HEADS = 16
SWA_KV_HEADS = 2
SWA_GROUP = SWA_Q_HEADS // SWA_KV_HEADS
SWA_WIDTH = SWA_Q_HEADS * HEAD_DIM
SWA_KV_WIDTH = SWA_KV_HEADS * HEAD_DIM
ROPE_THETA = 10000.0
RMS_EPS = 1e-6
LOG2E = 1.4426950408889634
Q_SCALE = HEAD_DIM ** -0.5 * LOG2E

LANES = 128
PAIR = LANES // HEAD_DIM
SB_PAIRS = SB_HEADS // PAIR
KV_DUP_WIDTH = SWA_KV_HEADS * LANES

SUB = 64
WIN = 256
LOOKBACK = WIN - SUB

F32_EXP2_UNDERFLOW = -151.0
MASKED_SCORE = -1e30

_C_SBQ = 0
_C_SBK = _C_SBQ + SB_WIDTH
_C_SBV = _C_SBK + SB_WIDTH
_C_SWQ = _C_SBV + SB_WIDTH
_C_SWK = _C_SWQ + SWA_WIDTH
_C_SWV = _C_SWK + KV_DUP_WIDTH
_C_ZSB = _C_SWV + KV_DUP_WIDTH
_C_ZSW = _C_ZSB + SB_WIDTH
_C_GSB = _C_ZSW + SWA_WIDTH
_C_GSW = _C_GSB + D_MODEL
_C_END = _C_GSW + D_MODEL

PROJ_TILE_ROWS = 512
OUT_TILE_ROWS = 512
VMEM_LIMIT_BYTES = 56 * 1024 * 1024


def _sigmoid(z):
    return 1.0 / (1.0 + jnp.exp(-z))


def _rope_tile(x, cos, sin_signed, first_half):
    partner = jnp.where(first_half, pltpu.roll(x, LANES - HEAD_DIM // 2, 1), pltpu.roll(x, HEAD_DIM // 2, 1))
    return x * cos + partner * sin_signed


def _proj_kernel(x_ref, gain_ref, w_ref, cos_ref, sin_ref,
                 sbq_ref, sbk_ref, sbv_ref, swq_ref, swk_ref, swv_ref,
                 asb_ref, asw_ref, ssb_ref, ssw_ref):
    x = x_ref[...]
    ms = jnp.mean(x * x, axis=-1, keepdims=True)
    xn = (x * lax.rsqrt(ms + RMS_EPS) * gain_ref[...]).astype(BF16)

    def mm(c0, width):
        return jnp.dot(xn, w_ref[:, c0:c0 + width], preferred_element_type=F32)

    cos = cos_ref[...]
    sin_signed = sin_ref[...]
    lane = lax.broadcasted_iota(jnp.int32, cos.shape, 1)
    first_half = (lane % HEAD_DIM) < (HEAD_DIM // 2)

    def roped(acc, scale):
        tiles = []
        for t in range(acc.shape[1] // LANES):
            r = _rope_tile(acc[:, t * LANES:(t + 1) * LANES], cos, sin_signed, first_half)
            tiles.append(r * scale if scale != 1.0 else r)
        return jnp.concatenate(tiles, axis=1)

    sbq_ref[...] = (mm(_C_SBQ, SB_WIDTH) * Q_SCALE).astype(BF16)
    sbk_ref[...] = mm(_C_SBK, SB_WIDTH).astype(BF16)
    sbv_ref[...] = mm(_C_SBV, SB_WIDTH).astype(BF16)
    half = SWA_WIDTH // 2
    for c in range(2):
        swq_ref[:, c * half:(c + 1) * half] = roped(mm(_C_SWQ + c * half, half), Q_SCALE).astype(BF16)
    swk_ref[...] = roped(mm(_C_SWK, KV_DUP_WIDTH), 1.0).astype(BF16)
    swv_ref[...] = mm(_C_SWV, KV_DUP_WIDTH).astype(BF16)
    z = mm(_C_ZSB, SB_WIDTH)
    asb_ref[...] = (z * _sigmoid(z)).astype(BF16)
    for c in range(2):
        z = mm(_C_ZSW + c * half, half)
        asw_ref[:, c * half:(c + 1) * half] = (z * _sigmoid(z)).astype(BF16)
    for c in range(2):
        ssb_ref[:, c * half:(c + 1) * half] = _sigmoid(mm(_C_GSB + c * half, half)).astype(BF16)
    for c in range(2):
        ssw_ref[:, c * half:(c + 1) * half] = _sigmoid(mm(_C_GSW + c * half, half)).astype(BF16)


def _proj_call(x2d, gain, w, cos, sin_signed, tile_rows):
    n = x2d.shape[0]
    table_tiles = cos.shape[0] // tile_rows
    widths = (SB_WIDTH, SB_WIDTH, SB_WIDTH, SWA_WIDTH, KV_DUP_WIDTH, KV_DUP_WIDTH,
              SB_WIDTH, SWA_WIDTH, D_MODEL, D_MODEL)
    row_tile = lambda width: pl.BlockSpec((tile_rows, width), lambda i: (i, 0))
    table = pl.BlockSpec((tile_rows, LANES), lambda i: (i % table_tiles, 0))
    return pl.pallas_call(
        _proj_kernel,
        grid=(n // tile_rows,),
        in_specs=[row_tile(D_MODEL),
                  pl.BlockSpec((1, D_MODEL), lambda i: (0, 0)),
                  pl.BlockSpec((D_MODEL, _C_END), lambda i: (0, 0), pipeline_mode=pl.Buffered(1)),
                  table, table],
        out_specs=[row_tile(wd) for wd in widths],
        out_shape=[jax.ShapeDtypeStruct((n, wd), BF16) for wd in widths],
        compiler_params=pltpu.CompilerParams(dimension_semantics=("arbitrary",),
                                             vmem_limit_bytes=VMEM_LIMIT_BYTES),
        name="proj",
    )(x2d, gain, w, cos, sin_signed)


def _log2_beta_terms(z):
    tail = jnp.log2(1.0 + jnp.exp2(-jnp.abs(z)))
    log_beta = jnp.minimum(z, 0.0) - tail
    return log_beta, log_beta - z


def _split_bf16(x):
    hi = x.astype(BF16)
    return hi, (x - hi.astype(F32)).astype(BF16)


def _sb_kernel(q_ref, k_ref, v_ref, km_ref, vm_ref, later_ref, a_ref, o_ref):
    i = pl.program_id(1)
    row = lax.broadcasted_iota(jnp.int32, (BLOCK, LANES), 0)
    col = lax.broadcasted_iota(jnp.int32, (BLOCK, LANES), 1)
    sub_row = row & (SUB - 1)
    low_half = lax.broadcasted_iota(jnp.int32, (SUB, LANES), 1) < HEAD_DIM
    later_blk = jnp.concatenate([later_ref[0:BLOCK, 0:BLOCK], later_ref[WIN:WIN + BLOCK, 0:BLOCK]], axis=0)
    items = [(s, p) for s in range(BLOCK // SUB) for p in range(SB_PAIRS)]

    q = q_ref[...]
    zero = jnp.zeros((SUB, LANES), BF16)
    q_stack = []
    for s, p in items:
        qp = q[s * SUB:(s + 1) * SUB, p * LANES:(p + 1) * LANES]
        q_stack.append(jnp.concatenate([jnp.where(low_half, qp, zero), jnp.where(low_half, zero, qp)], axis=0))

    def unstack(res):
        return jnp.where(low_half, res[:SUB], res[SUB:])

    def key_block(kb, vb, masks, carry, acc):
        new_carry, new_acc = [], []
        for n, (s, p) in enumerate(items):
            kp = kb[:, p * LANES:(p + 1) * LANES]
            vp = vb[:, p * LANES:(p + 1) * LANES]
            z = lax.dot_general(q_stack[n], kp, (((1,), (1,)), ((), ())), preferred_element_type=F32)
            log_beta, log_1m = _log2_beta_terms(z)
            if masks[s] is not None:
                log_1m = jnp.where(masks[s], log_1m, 0.0)
            hi, lo = _split_bf16(log_1m)
            suffix = jnp.dot(jnp.concatenate([hi, lo], axis=1), later_blk, preferred_element_type=F32)
            w = jnp.exp2(log_beta + suffix + carry[n])
            if masks[s] is not None:
                w = jnp.where(masks[s], w, 0.0)
            new_acc.append(acc[n] + unstack(jnp.dot(w.astype(BF16), vp, preferred_element_type=F32)))
            new_carry.append(carry[n] + jnp.sum(log_1m, axis=1, keepdims=True))
        return new_carry, new_acc

    def least_decayed(carry):
        m = carry[0]
        for c in carry[1:]:
            m = jnp.maximum(m, c)
        return jnp.max(m)

    def older_blocks(first_block, first_masks, carry, acc):
        s0 = pl.multiple_of(first_block * BLOCK, BLOCK)
        carry, acc = key_block(k_ref[pl.ds(s0, BLOCK), :], v_ref[pl.ds(s0, BLOCK), :], first_masks, carry, acc)

        def cond(state):
            j, top, _, _ = state
            return jnp.logical_and(j >= 0, top >= F32_EXP2_UNDERFLOW)

        def body(state):
            j, _, carry, acc = state
            sj = pl.multiple_of(j * BLOCK, BLOCK)
            carry, acc = key_block(k_ref[pl.ds(sj, BLOCK), :], v_ref[pl.ds(sj, BLOCK), :], (None, None), carry, acc)
            return j - 1, least_decayed(carry), carry, acc

        _, top, carry, acc = lax.while_loop(cond, body, (first_block - 1, least_decayed(carry), carry, acc))
        meta_valid = col >= PAD

        def with_meta(carry, acc):
            return key_block(km_ref[...], vm_ref[...], (meta_valid, meta_valid), carry, acc)[1]

        return lax.cond(top >= F32_EXP2_UNDERFLOW, with_meta, lambda carry, acc: acc, carry, acc)

    def from_own_block():
        carry = [jnp.zeros((BLOCK, 1), F32) for _ in items]
        acc = [jnp.zeros((SUB, LANES), F32) for _ in items]
        return older_blocks(i, (col < sub_row, col < sub_row + SUB), carry, acc)

    def near_window_first():
        visible = col < sub_row + (LOOKBACK - LANES)
        split, log_betas, oldest, starts = [], [], [], []
        for n, (s, p) in enumerate(items):
            start = pl.multiple_of(i * BLOCK - LOOKBACK + s * SUB, SUB)
            kw = k_ref[pl.ds(start, WIN), p * LANES:(p + 1) * LANES]
            z = lax.dot_general(q_stack[n], kw, (((1,), (1,)), ((), ())), preferred_element_type=F32)
            log_beta, log_1m = _log2_beta_terms(z)
            log_1m = jnp.concatenate([log_1m[:, :LANES], jnp.where(visible, log_1m[:, LANES:], 0.0)], axis=1)
            hi, lo = _split_bf16(log_1m)
            split.append(jnp.concatenate([hi, lo], axis=1))
            log_betas.append(log_beta)
            oldest.append(log_1m[:, 0:1])
            starts.append(start)
        suffix_all = jnp.dot(jnp.concatenate(split, axis=0), later_ref[...], preferred_element_type=F32)
        carry, acc = [], []
        for n, (s, p) in enumerate(items):
            suffix = suffix_all[n * BLOCK:(n + 1) * BLOCK]
            w = jnp.exp2(log_betas[n] + suffix)
            w = jnp.concatenate([w[:, :LANES], jnp.where(visible, w[:, LANES:], 0.0)], axis=1)
            vw = v_ref[pl.ds(starts[n], WIN), p * LANES:(p + 1) * LANES]
            acc.append(unstack(jnp.dot(w.astype(BF16), vw, preferred_element_type=F32)))
            carry.append(suffix[:, 0:1] + oldest[n])
        return lax.cond(least_decayed(carry) >= F32_EXP2_UNDERFLOW,
                        lambda carry, acc: older_blocks(i - 2, (col < SUB, None), carry, acc),
                        lambda carry, acc: acc, carry, acc)

    acc = lax.cond(i * BLOCK >= LOOKBACK, near_window_first, from_own_block)
    for n, (s, p) in enumerate(items):
        gate = a_ref[s * SUB:(s + 1) * SUB, p * LANES:(p + 1) * LANES].astype(F32)
        o_ref[s * SUB:(s + 1) * SUB, p * LANES:(p + 1) * LANES] = (acc[n] * gate).astype(BF16)


def _sb_call(q, k, v, k_meta, v_meta, later, act):
    b, seq, width = q.shape
    q_tile = pl.BlockSpec((None, BLOCK, width), lambda bi, i: (bi, i, 0))
    whole = pl.BlockSpec((None, seq, width), lambda bi, i: (bi, 0, 0), pipeline_mode=pl.Buffered(1))
    meta = pl.BlockSpec((BLOCK, width), lambda bi, i: (0, 0))
    return pl.pallas_call(
        _sb_kernel,
        grid=(b, seq // BLOCK),
        in_specs=[q_tile, whole, whole, meta, meta, pl.BlockSpec(later.shape, lambda bi, i: (0, 0)), q_tile],
        out_specs=q_tile,
        out_shape=jax.ShapeDtypeStruct((b, seq, width), BF16),
        compiler_params=pltpu.CompilerParams(dimension_semantics=("arbitrary", "arbitrary"),
                                             vmem_limit_bytes=VMEM_LIMIT_BYTES),
        name="sb_attn",
    )(q, k, v, k_meta, v_meta, later, act)


def _swa_kernel(sink_ref, q_ref, kp_ref, ko_ref, vp_ref, vo_ref, km_ref, vm_ref, a_ref, o_ref):
    i = pl.program_id(1)
    first = i == 0
    k_cat = jnp.concatenate([jnp.where(first, km_ref[...], kp_ref[...]), ko_ref[...]], axis=0)
    v_cat = jnp.concatenate([jnp.where(first, vm_ref[...], vp_ref[...]), vo_ref[...]], axis=0)
    t = lax.broadcasted_iota(jnp.int32, (BLOCK, 2 * BLOCK), 0)
    c = lax.broadcasted_iota(jnp.int32, (BLOCK, 2 * BLOCK), 1)
    first_valid = jnp.where(first, PAD, 0)
    mask = (c > t) & (c <= t + BLOCK) & (c >= first_valid)
    low_half = lax.broadcasted_iota(jnp.int32, (BLOCK, LANES), 1) < HEAD_DIM
    zero = jnp.zeros((BLOCK, LANES), BF16)
    ones = jnp.ones((2 * BLOCK, LANES), BF16)
    pairs_per_group = SWA_GROUP // PAIR
    for g in range(SWA_KV_HEADS):
        kg = k_cat[:, g * LANES:(g + 1) * LANES]
        vg = jnp.concatenate([v_cat[:, g * LANES:(g + 1) * LANES], ones], axis=1)
        q_rows = []
        for pp in range(pairs_per_group):
            p = g * pairs_per_group + pp
            qp = q_ref[:, p * LANES:(p + 1) * LANES]
            q_rows += [jnp.where(low_half, qp, zero), jnp.where(low_half, zero, qp)]
        s_all = lax.dot_general(jnp.concatenate(q_rows, axis=0), kg, (((1,), (1,)), ((), ())),
                                preferred_element_type=F32)
        weights, sink_terms = [], []
        for n in range(SWA_GROUP):
            s = jnp.where(mask, s_all[n * BLOCK:(n + 1) * BLOCK], MASKED_SCORE)
            sink = sink_ref[g * SWA_GROUP + n] * LOG2E
            m = jnp.maximum(jnp.max(s, axis=1, keepdims=True), sink)
            weights.append(jnp.exp2(s - m).astype(BF16))
            sink_terms.append(jnp.exp2(sink - m))
        pv = jnp.dot(jnp.concatenate(weights, axis=0), vg, preferred_element_type=F32)
        for pp in range(pairs_per_group):
            outs = []
            for hh in range(PAIR):
                n = pp * PAIR + hh
                blk = pv[n * BLOCK:(n + 1) * BLOCK]
                outs.append(blk[:, :LANES] * (1.0 / (blk[:, LANES:] + sink_terms[n])))
            p = g * pairs_per_group + pp
            gate = a_ref[:, p * LANES:(p + 1) * LANES].astype(F32)
            o_ref[:, p * LANES:(p + 1) * LANES] = (jnp.where(low_half, outs[0], outs[1]) * gate).astype(BF16)


def _swa_call(sinks, q, k, v, k_meta, v_meta, act):
    b, seq, width = q.shape
    kvw = k.shape[2]
    q_tile = pl.BlockSpec((None, BLOCK, width), lambda bi, i: (bi, i, 0))
    kv_own = pl.BlockSpec((None, BLOCK, kvw), lambda bi, i: (bi, i, 0))
    kv_prev = pl.BlockSpec((None, BLOCK, kvw), lambda bi, i: (bi, jnp.maximum(i - 1, 0), 0))
    meta = pl.BlockSpec((BLOCK, kvw), lambda bi, i: (0, 0))
    return pl.pallas_call(
        _swa_kernel,
        grid=(b, seq // BLOCK),
        in_specs=[pl.BlockSpec(memory_space=pltpu.SMEM),
                  q_tile, kv_prev, kv_own, kv_prev, kv_own, meta, meta, q_tile],
        out_specs=q_tile,
        out_shape=jax.ShapeDtypeStruct((b, seq, width), BF16),
        compiler_params=pltpu.CompilerParams(dimension_semantics=("arbitrary", "arbitrary")),
        name="swa_attn",
    )(sinks, q, k, k, v, v, k_meta, v_meta, act)


def _out_kernel(x_ref, asb_ref, asw_ref, ssb_ref, ssw_ref, wsb_ref, wsw_ref, wout_ref, gain_ref, o_ref):
    y_sb = jnp.dot(asb_ref[...], wsb_ref[...], preferred_element_type=F32)
    y_sw = jnp.dot(asw_ref[...], wsw_ref[...], preferred_element_type=F32)
    merged = ssb_ref[...].astype(F32) * y_sb + ssw_ref[...].astype(F32) * y_sw
    h = x_ref[...] + jnp.dot(merged.astype(BF16), wout_ref[...], preferred_element_type=F32)
    ms = jnp.mean(h * h, axis=-1, keepdims=True)
    o_ref[...] = h * lax.rsqrt(ms + RMS_EPS) * gain_ref[...]


def _out_call(x2d, act_sb, act_sw, sig_sb, sig_sw, w_sb, w_sw, w_out, gain, tile_rows):
    n = x2d.shape[0]
    row_tile = lambda width: pl.BlockSpec((tile_rows, width), lambda i: (i, 0))
    const = lambda shape: pl.BlockSpec(shape, lambda i: (0, 0))
    return pl.pallas_call(
        _out_kernel,
        grid=(n // tile_rows,),
        in_specs=[row_tile(D_MODEL), row_tile(SB_WIDTH), row_tile(SWA_WIDTH), row_tile(D_MODEL),
                  row_tile(D_MODEL), const(w_sb.shape), const(w_sw.shape), const(w_out.shape),
                  const((1, D_MODEL))],
        out_specs=row_tile(D_MODEL),
        out_shape=jax.ShapeDtypeStruct((n, D_MODEL), F32),
        compiler_params=pltpu.CompilerParams(dimension_semantics=("arbitrary",),
                                             vmem_limit_bytes=VMEM_LIMIT_BYTES),
        name="out_proj",
    )(x2d, act_sb, act_sw, sig_sb, sig_sw, w_sb, w_sw, w_out, gain)


def _rope_tables(pos):
    half = HEAD_DIM // 2
    inv = ROPE_THETA ** (-jnp.arange(half, dtype=F32) / half)
    ang = pos.astype(F32)[:, None] * inv[None, :]
    cos, sin = jnp.cos(ang), jnp.sin(ang)
    reps = LANES // HEAD_DIM
    return (jnp.tile(jnp.concatenate([cos, cos], axis=1), (1, reps)),
            jnp.tile(jnp.concatenate([-sin, sin], axis=1), (1, reps)))


def _relayout_w_in(w):
    offs = [0]
    for width in (SB_WIDTH, SB_WIDTH, SB_WIDTH, SWA_WIDTH, SWA_KV_WIDTH, SWA_KV_WIDTH,
                  SB_WIDTH, SWA_WIDTH, D_MODEL, D_MODEL):
        offs.append(offs[-1] + width)
    head = lambda base, g: w[:, base + g * HEAD_DIM: base + (g + 1) * HEAD_DIM]
    dup = lambda base: [head(base, g) for g in range(SWA_KV_HEADS) for _ in range(PAIR)]
    parts = [w[:, :offs[4]]] + dup(offs[4]) + dup(offs[5]) + [w[:, offs[6]:]]
    return jnp.concatenate(parts, axis=1).astype(BF16)


def _later_key_selector():
    j = jnp.arange(WIN)
    later = (j[:, None] > j[None, :]).astype(BF16)
    return jnp.concatenate([later, later], axis=0)


def kernel(x, meta_tokens, norm_gain, w_in, w_branch_sb, w_branch_swa, w_out, attn_sinks, final_norm_gain):
    b, seq, d = x.shape
    assert d == D_MODEL and norm_gain.shape[0] == 1 and seq % PROJ_TILE_ROWS == 0
    n = b * seq
    x2d = x.reshape(n, d)
    w = _relayout_w_in(w_in[0])
    assert w.shape[1] == _C_END
    gain = norm_gain[0].reshape(1, d)

    cos, sin_signed = _rope_tables(jnp.arange(seq) + N_META)
    tok = _proj_call(x2d, gain, w, cos, sin_signed, PROJ_TILE_ROWS)
    sbq, sbk, sbv, swq, swk, swv, act_sb, act_sw, sig_sb, sig_sw = tok

    meta_tile = jnp.concatenate([jnp.zeros((PAD, d), x.dtype), meta_tokens.astype(x.dtype)], axis=0)
    mcos, msin = _rope_tables(jnp.maximum(jnp.arange(BLOCK) - PAD, 0))
    meta = _proj_call(meta_tile, gain, w, mcos, msin, BLOCK)
    sbk_m, sbv_m, swk_m, swv_m = meta[1], meta[2], meta[4], meta[5]

    r3 = lambda a: a.reshape(b, seq, a.shape[1])
    gated_sb = _sb_call(r3(sbq), r3(sbk), r3(sbv), sbk_m, sbv_m, _later_key_selector(), r3(act_sb))
    gated_sw = _swa_call(attn_sinks[0].astype(F32), r3(swq), r3(swk), r3(swv), swk_m, swv_m, r3(act_sw))

    out = _out_call(x2d, gated_sb.reshape(n, SB_WIDTH), gated_sw.reshape(n, SWA_WIDTH), sig_sb, sig_sw,
                    w_branch_sb[0].astype(BF16), w_branch_swa[0].astype(BF16), w_out[0].astype(BF16),
                    final_norm_gain.reshape(1, d), OUT_TILE_ROWS)
    return out.reshape(b, seq, d)
```

```python
import jax
import jax.numpy as jnp
from jax import lax
from jax.experimental import pallas as pl
from jax.experimental.pallas import tpu as pltpu

F32 = jnp.float32
BF16 = jnp.bfloat16

D_MODEL = 1024
N_META = 16
BLOCK = 128
PAD = BLOCK - N_META
HEAD_DIM = 64
SB_HEADS = 8
SB_WIDTH = SB_HEADS * HEAD_DIM
SWA_Q_HEADS = 16
SWA_KV_HEADS = 2
SWA_GROUP = SWA_Q_HEADS // SWA_KV_HEADS
SWA_WIDTH = SWA_Q_HEADS * HEAD_DIM
SWA_KV_WIDTH = SWA_KV_HEADS * HEAD_DIM
ROPE_THETA = 10000.0
RMS_EPS = 1e-6
LOG2E = 1.4426950408889634
Q_SCALE = HEAD_DIM ** -0.5 * LOG2E

LANES = 128
PAIR = LANES // HEAD_DIM
SB_PAIRS = SB_HEADS // PAIR
KV_DUP_WIDTH = SWA_KV_HEADS * LANES
SWA_STACK = 8
ATTN_ROWS = 256

SUB = 64
WIN = 256
LOOKBACK = WIN - SUB

F32_EXP2_UNDERFLOW = -151.0
MASKED_SCORE = -1e30

_C_SBQ = 0
_C_SBK = _C_SBQ + SB_WIDTH
_C_SBV = _C_SBK + SB_WIDTH
_C_SWQ = _C_SBV + SB_WIDTH
_C_SWK = _C_SWQ + SWA_WIDTH
_C_SWV = _C_SWK + KV_DUP_WIDTH
_C_ZSB = _C_SWV + KV_DUP_WIDTH
_C_ZSW = _C_ZSB + SB_WIDTH
_C_GSB = _C_ZSW + SWA_WIDTH
_C_GSW = _C_GSB + D_MODEL
_C_END = _C_GSW + D_MODEL

PROJ_TILE_ROWS = 512
OUT_TILE_ROWS = 1024
VMEM_LIMIT_BYTES = 56 * 1024 * 1024


def _sigmoid(z):
    return 1.0 / (1.0 + jnp.exp(-z))


def _rope_tile(x, cos, sin_signed, first_half):
    partner = jnp.where(first_half, pltpu.roll(x, LANES - HEAD_DIM // 2, 1), pltpu.roll(x, HEAD_DIM // 2, 1))
    return x * cos + partner * sin_signed


def _proj_kernel(x_ref, gain_ref, w_ref, cos_ref, sin_ref,
                 sbq_ref, sbk_ref, sbv_ref, swq_ref, swk_ref, swv_ref,
                 asb_ref, asw_ref, ssb_ref, ssw_ref):
    x = x_ref[...]
    ms = jnp.mean(x * x, axis=-1, keepdims=True)
    xn = (x * lax.rsqrt(ms + RMS_EPS) * gain_ref[...]).astype(BF16)

    def mm(c0, width):
        return jnp.dot(xn, w_ref[:, c0:c0 + width], preferred_element_type=F32)

    cos = cos_ref[...]
    sin_signed = sin_ref[...]
    lane = lax.broadcasted_iota(jnp.int32, cos.shape, 1)
    first_half = (lane % HEAD_DIM) < (HEAD_DIM // 2)

    def roped(acc, scale):
        tiles = []
        for t in range(acc.shape[1] // LANES):
            r = _rope_tile(acc[:, t * LANES:(t + 1) * LANES], cos, sin_signed, first_half)
            tiles.append(r * scale if scale != 1.0 else r)
        return jnp.concatenate(tiles, axis=1)

    low_half = lane < HEAD_DIM
    zero = jnp.zeros(cos.shape, BF16)

    def store_per_head(ref, first_tile, q):
        for t in range(q.shape[1] // LANES):
            tile = q[:, t * LANES:(t + 1) * LANES]
            at = (first_tile + t) * PAIR * LANES
            ref[:, at:at + LANES] = jnp.where(low_half, tile, zero)
            ref[:, at + LANES:at + 2 * LANES] = jnp.where(low_half, zero, tile)

    store_per_head(sbq_ref, 0, (mm(_C_SBQ, SB_WIDTH) * Q_SCALE).astype(BF16))
    sbk_ref[...] = mm(_C_SBK, SB_WIDTH).astype(BF16)
    sbv_ref[...] = mm(_C_SBV, SB_WIDTH).astype(BF16)
    half = SWA_WIDTH // 2
    for c in range(2):
        store_per_head(swq_ref, c * (half // LANES), roped(mm(_C_SWQ + c * half, half), Q_SCALE).astype(BF16))
    swk_ref[...] = roped(mm(_C_SWK, KV_DUP_WIDTH), 1.0).astype(BF16)
    swv_ref[...] = mm(_C_SWV, KV_DUP_WIDTH).astype(BF16)
    z = mm(_C_ZSB, SB_WIDTH)
    asb_ref[...] = (z * _sigmoid(z)).astype(BF16)
    for c in range(2):
        z = mm(_C_ZSW + c * half, half)
        asw_ref[:, c * half:(c + 1) * half] = (z * _sigmoid(z)).astype(BF16)
    for c in range(2):
        ssb_ref[:, c * half:(c + 1) * half] = _sigmoid(mm(_C_GSB + c * half, half)).astype(BF16)
    for c in range(2):
        ssw_ref[:, c * half:(c + 1) * half] = _sigmoid(mm(_C_GSW + c * half, half)).astype(BF16)


def _proj_call(x2d, gain, w, cos, sin_signed, tile_rows):
    n = x2d.shape[0]
    table_tiles = cos.shape[0] // tile_rows
    widths = (PAIR * SB_WIDTH, SB_WIDTH, SB_WIDTH, PAIR * SWA_WIDTH, KV_DUP_WIDTH, KV_DUP_WIDTH,
              SB_WIDTH, SWA_WIDTH, D_MODEL, D_MODEL)
    row_tile = lambda width: pl.BlockSpec((tile_rows, width), lambda i: (i, 0))
    table = pl.BlockSpec((tile_rows, LANES), lambda i: (i % table_tiles, 0))
    return pl.pallas_call(
        _proj_kernel,
        grid=(n // tile_rows,),
        in_specs=[row_tile(D_MODEL),
                  pl.BlockSpec((1, D_MODEL), lambda i: (0, 0)),
                  pl.BlockSpec((D_MODEL, _C_END), lambda i: (0, 0), pipeline_mode=pl.Buffered(1)),
                  table, table],
        out_specs=[row_tile(wd) for wd in widths],
        out_shape=[jax.ShapeDtypeStruct((n, wd), BF16) for wd in widths],
        compiler_params=pltpu.CompilerParams(dimension_semantics=("arbitrary",),
                                             vmem_limit_bytes=VMEM_LIMIT_BYTES),
        name="proj",
    )(x2d, gain, w, cos, sin_signed)


def _log2_beta_terms(z):
    tail = jnp.log2(1.0 + jnp.exp2(-jnp.abs(z)))
    log_beta = jnp.minimum(z, 0.0) - tail
    return log_beta, log_beta - z


def _sb_kernel(q_ref, k_ref, v_ref, km_ref, vm_ref, later_ref, a_ref, o_ref):
    i = pl.program_id(1)
    row = lax.broadcasted_iota(jnp.int32, (BLOCK, LANES), 0)
    col = lax.broadcasted_iota(jnp.int32, (BLOCK, LANES), 1)
    sub_row = row & (SUB - 1)
    low_half = lax.broadcasted_iota(jnp.int32, (SUB, LANES), 1) < HEAD_DIM
    later_blk = later_ref[0:BLOCK, 0:BLOCK]
    subs = range(ATTN_ROWS // SUB)
    items = [(s, p) for s in subs for p in range(SB_PAIRS)]

    q_stack = [jnp.concatenate([q_ref[s * SUB:(s + 1) * SUB, (PAIR * p + hh) * LANES:(PAIR * p + hh + 1) * LANES]
                                for hh in range(PAIR)], axis=0) for s, p in items]

    def unstack(res):
        return jnp.where(low_half, res[:SUB], res[SUB:])

    def causal(offset):
        if offset <= -SUB:
            return False
        return None if offset >= BLOCK else col < sub_row + offset

    def prefix(limit):
        if limit <= 0:
            return False
        return None if limit >= BLOCK else col < limit

    def key_block(kb, vb, masks, carry, acc):
        new_carry, new_acc = [], []
        for n, (s, p) in enumerate(items):
            if masks[s] is False:
                new_carry.append(carry[n])
                new_acc.append(acc[n])
                continue
            kp = kb[:, p * LANES:(p + 1) * LANES]
            vp = vb[:, p * LANES:(p + 1) * LANES]
            z = lax.dot_general(q_stack[n], kp, (((1,), (1,)), ((), ())), preferred_element_type=F32)
            log_beta, log_1m = _log2_beta_terms(z)
            if masks[s] is not None:
                log_1m = jnp.where(masks[s], log_1m, 0.0)
            suffix = jnp.dot(log_1m.astype(BF16), later_blk, preferred_element_type=F32)
            w = jnp.exp2(log_beta + suffix + carry[n])
            if masks[s] is not None:
                w = jnp.where(masks[s], w, 0.0)
            new_acc.append(acc[n] + unstack(jnp.dot(w.astype(BF16), vp, preferred_element_type=F32)))
            new_carry.append(carry[n] + jnp.sum(log_1m, axis=1, keepdims=True))
        return new_carry, new_acc

    def least_decayed(carry):
        m = carry[0]
        for c in carry[1:]:
            m = jnp.maximum(m, c)
        return jnp.max(m)

    def older_blocks(peeled, loop_from, carry, acc):
        for blk, masks in peeled:
            s0 = blk * BLOCK if isinstance(blk, int) else pl.multiple_of(blk * BLOCK, BLOCK)
            carry, acc = key_block(k_ref[pl.ds(s0, BLOCK), :], v_ref[pl.ds(s0, BLOCK), :], masks, carry, acc)

        def cond(state):
            j, top, _, _ = state
            return jnp.logical_and(j >= 0, top >= F32_EXP2_UNDERFLOW)

        def body(state):
            j, _, carry, acc = state
            sj = pl.multiple_of(j * BLOCK, BLOCK)
            carry, acc = key_block(k_ref[pl.ds(sj, BLOCK), :], v_ref[pl.ds(sj, BLOCK), :],
                                   tuple(None for _ in subs), carry, acc)
            return j - 1, least_decayed(carry), carry, acc

        _, top, carry, acc = lax.while_loop(cond, body, (loop_from, least_decayed(carry), carry, acc))
        meta_valid = col >= PAD

        def with_meta(carry, acc):
            return key_block(km_ref[...], vm_ref[...], tuple(meta_valid for _ in subs), carry, acc)[1]

        return lax.cond(top >= F32_EXP2_UNDERFLOW, with_meta, lambda carry, acc: acc, carry, acc)

    def first_step():
        carry = [jnp.zeros((BLOCK, 1), F32) for _ in items]
        acc = [jnp.zeros((SUB, LANES), F32) for _ in items]
        peeled = [(b, tuple(causal(s * SUB - b * BLOCK) for s in subs))
                  for b in reversed(range(ATTN_ROWS // BLOCK))]
        return older_blocks(peeled, jnp.int32(-1), carry, acc)

    def near_window_first():
        visible = col < sub_row + (LOOKBACK - LANES)
        split, log_betas, oldest, starts = [], [], [], []
        for n, (s, p) in enumerate(items):
            start = pl.multiple_of(i * ATTN_ROWS - LOOKBACK + s * SUB, SUB)
            kw = k_ref[pl.ds(start, WIN), p * LANES:(p + 1) * LANES]
            z = lax.dot_general(q_stack[n], kw, (((1,), (1,)), ((), ())), preferred_element_type=F32)
            log_beta, log_1m = _log2_beta_terms(z)
            log_1m = jnp.concatenate([log_1m[:, :LANES], jnp.where(visible, log_1m[:, LANES:], 0.0)], axis=1)
            split.append(log_1m.astype(BF16))
            log_betas.append(log_beta)
            oldest.append(log_1m[:, 0:1])
            starts.append(start)
        suffix_all = jnp.dot(jnp.concatenate(split, axis=0), later_ref[...], preferred_element_type=F32)
        carry, acc = [], []
        for n, (s, p) in enumerate(items):
            suffix = suffix_all[n * BLOCK:(n + 1) * BLOCK]
            w = jnp.exp2(log_betas[n] + suffix)
            w = jnp.concatenate([w[:, :LANES], jnp.where(visible, w[:, LANES:], 0.0)], axis=1)
            vw = v_ref[pl.ds(starts[n], WIN), p * LANES:(p + 1) * LANES]
            acc.append(unstack(jnp.dot(w.astype(BF16), vw, preferred_element_type=F32)))
            carry.append(suffix[:, 0:1] + oldest[n])

        def keys_before_windows(carry, acc):
            first = i * (ATTN_ROWS // BLOCK)
            peeled, back = [], 1
            while back * BLOCK - LOOKBACK < BLOCK:
                peeled.append((first - back, tuple(prefix(s * SUB - LOOKBACK + back * BLOCK) for s in subs)))
                back += 1
            return older_blocks(peeled, first - back, carry, acc)

        return lax.cond(least_decayed(carry) >= F32_EXP2_UNDERFLOW, keys_before_windows,
                        lambda carry, acc: acc, carry, acc)

    acc = lax.cond(i * ATTN_ROWS >= LOOKBACK, near_window_first, first_step)
    for n, (s, p) in enumerate(items):
        gate = a_ref[s * SUB:(s + 1) * SUB, p * LANES:(p + 1) * LANES].astype(F32)
        o_ref[s * SUB:(s + 1) * SUB, p * LANES:(p + 1) * LANES] = (acc[n] * gate).astype(BF16)


def _sb_call(q, k, v, k_meta, v_meta, later, act):
    b, seq, width = k.shape
    q_tile = pl.BlockSpec((None, ATTN_ROWS, width), lambda bi, i: (bi, i, 0))
    q_per_head = pl.BlockSpec((None, ATTN_ROWS, q.shape[2]), lambda bi, i: (bi, i, 0))
    whole = pl.BlockSpec((None, seq, width), lambda bi, i: (bi, 0, 0), pipeline_mode=pl.Buffered(1))
    meta = pl.BlockSpec((BLOCK, width), lambda bi, i: (0, 0))
    return pl.pallas_call(
        _sb_kernel,
        grid=(b, seq // ATTN_ROWS),
        in_specs=[q_per_head, whole, whole, meta, meta, pl.BlockSpec(later.shape, lambda bi, i: (0, 0)), q_tile],
        out_specs=q_tile,
        out_shape=jax.ShapeDtypeStruct((b, seq, width), BF16),
        compiler_params=pltpu.CompilerParams(dimension_semantics=("arbitrary", "arbitrary"),
                                             vmem_limit_bytes=VMEM_LIMIT_BYTES),
        name="sb_attn",
    )(q, k, v, k_meta, v_meta, later, act)


def _swa_kernel(sink_ref, q_ref, kp_ref, ko_ref, vp_ref, vo_ref, km_ref, vm_ref, a_ref, o_ref):
    i = pl.program_id(1)
    first = i == 0
    t = lax.broadcasted_iota(jnp.int32, (BLOCK, 2 * BLOCK), 0)
    c = lax.broadcasted_iota(jnp.int32, (BLOCK, 2 * BLOCK), 1)
    band = (c > t) & (c <= t + BLOCK)
    sink_slot = lax.broadcasted_iota(jnp.int32, (1, 2 * BLOCK), 1) == 0
    low_half = lax.broadcasted_iota(jnp.int32, (BLOCK, LANES), 1) < HEAD_DIM
    ones = jnp.ones((2 * BLOCK, LANES), BF16)
    packed_rows = 16
    top_row = lax.broadcasted_iota(jnp.int32, (packed_rows, KV_DUP_WIDTH), 0) == 0
    for blk in range(ATTN_ROWS // BLOCK):
        rows = slice(blk * BLOCK, (blk + 1) * BLOCK)
        if blk == 0:
            k_prev = jnp.where(first, km_ref[...], kp_ref[...])
            v_prev = jnp.where(first, vm_ref[...], vp_ref[...])
            mask = band & (c >= jnp.where(first, PAD, 0))
        else:
            k_prev = ko_ref[(blk - 1) * BLOCK:blk * BLOCK, :]
            v_prev = vo_ref[(blk - 1) * BLOCK:blk * BLOCK, :]
            mask = band
        k_cat = jnp.concatenate([k_prev, ko_ref[rows, :]], axis=0)
        v_head = jnp.where(top_row, jnp.zeros((packed_rows, KV_DUP_WIDTH), BF16), v_prev[:packed_rows])
        v_cat = jnp.concatenate([v_head, v_prev[packed_rows:], vo_ref[rows, :]], axis=0)
        for first_head in range(0, SWA_Q_HEADS, SWA_STACK):
            g = first_head // SWA_GROUP
            kg = k_cat[:, g * LANES:(g + 1) * LANES]
            vg = jnp.concatenate([v_cat[:, g * LANES:(g + 1) * LANES], ones], axis=1)
            q_rows = [q_ref[rows, (first_head + n) * LANES:(first_head + n + 1) * LANES] for n in range(SWA_STACK)]
            s_all = lax.dot_general(jnp.concatenate(q_rows, axis=0), kg, (((1,), (1,)), ((), ())),
                                    preferred_element_type=F32)
            weights = []
            for n in range(SWA_STACK):
                fill = jnp.where(sink_slot, sink_ref[first_head + n] * LOG2E, MASKED_SCORE)
                s = jnp.where(mask, s_all[n * BLOCK:(n + 1) * BLOCK], fill)
                weights.append(jnp.exp2(s - jnp.max(s, axis=1, keepdims=True)).astype(BF16))
            pv = jnp.dot(jnp.concatenate(weights, axis=0), vg, preferred_element_type=F32)
            for pp in range(SWA_STACK // PAIR):
                blk0 = pv[(pp * PAIR) * BLOCK:(pp * PAIR + 1) * BLOCK]
                blk1 = pv[(pp * PAIR + 1) * BLOCK:(pp * PAIR + 2) * BLOCK]
                num = jnp.where(low_half, blk0[:, :LANES], blk1[:, :LANES])
                den = jnp.where(low_half, blk0[:, LANES:], blk1[:, LANES:])
                p = first_head // PAIR + pp
                gate = a_ref[rows, p * LANES:(p + 1) * LANES].astype(F32)
                o_ref[rows, p * LANES:(p + 1) * LANES] = (num * (1.0 / den) * gate).astype(BF16)


def _swa_call(sinks, q, k, v, k_meta, v_meta, act):
    b, seq, width = act.shape
    kvw = k.shape[2]
    blocks_per_step = ATTN_ROWS // BLOCK
    q_tile = pl.BlockSpec((None, ATTN_ROWS, width), lambda bi, i: (bi, i, 0))
    q_per_head = pl.BlockSpec((None, ATTN_ROWS, q.shape[2]), lambda bi, i: (bi, i, 0))
    kv_own = pl.BlockSpec((None, ATTN_ROWS, kvw), lambda bi, i: (bi, i, 0))
    kv_prev = pl.BlockSpec((None, BLOCK, kvw), lambda bi, i: (bi, jnp.maximum(i * blocks_per_step - 1, 0), 0))
    meta = pl.BlockSpec((BLOCK, kvw), lambda bi, i: (0, 0))
    return pl.pallas_call(
        _swa_kernel,
        grid=(b, seq // ATTN_ROWS),
        in_specs=[pl.BlockSpec(memory_space=pltpu.SMEM),
                  q_per_head, kv_prev, kv_own, kv_prev, kv_own, meta, meta, q_tile],
        out_specs=q_tile,
        out_shape=jax.ShapeDtypeStruct((b, seq, width), BF16),
        compiler_params=pltpu.CompilerParams(dimension_semantics=("arbitrary", "arbitrary")),
        name="swa_attn",
    )(sinks, q, k, k, v, v, k_meta, v_meta, act)


def _out_kernel(x_ref, asb_ref, asw_ref, ssb_ref, ssw_ref, wsb_ref, wsw_ref, wout_ref, gain_ref, o_ref):
    y_sb = jnp.dot(asb_ref[...], wsb_ref[...], preferred_element_type=F32)
    y_sw = jnp.dot(asw_ref[...], wsw_ref[...], preferred_element_type=F32)
    merged = ssb_ref[...].astype(F32) * y_sb + ssw_ref[...].astype(F32) * y_sw
    h = x_ref[...] + jnp.dot(merged.astype(BF16), wout_ref[...], preferred_element_type=F32)
    ms = jnp.mean(h * h, axis=-1, keepdims=True)
    o_ref[...] = h * lax.rsqrt(ms + RMS_EPS) * gain_ref[...]


def _out_call(x2d, act_sb, act_sw, sig_sb, sig_sw, w_sb, w_sw, w_out, gain, tile_rows):
    n = x2d.shape[0]
    row_tile = lambda width: pl.BlockSpec((tile_rows, width), lambda i: (i, 0))
    const = lambda shape: pl.BlockSpec(shape, lambda i: (0, 0))
    return pl.pallas_call(
        _out_kernel,
        grid=(n // tile_rows,),
        in_specs=[row_tile(D_MODEL), row_tile(SB_WIDTH), row_tile(SWA_WIDTH), row_tile(D_MODEL),
                  row_tile(D_MODEL), const(w_sb.shape), const(w_sw.shape), const(w_out.shape),
                  const((1, D_MODEL))],
        out_specs=row_tile(D_MODEL),
        out_shape=jax.ShapeDtypeStruct((n, D_MODEL), F32),
        compiler_params=pltpu.CompilerParams(dimension_semantics=("arbitrary",),
                                             vmem_limit_bytes=VMEM_LIMIT_BYTES),
        name="out_proj",
    )(x2d, act_sb, act_sw, sig_sb, sig_sw, w_sb, w_sw, w_out, gain)


def _rope_tables(pos):
    half = HEAD_DIM // 2
    inv = ROPE_THETA ** (-jnp.arange(half, dtype=F32) / half)
    ang = pos.astype(F32)[:, None] * inv[None, :]
    cos, sin = jnp.cos(ang), jnp.sin(ang)
    reps = LANES // HEAD_DIM
    return (jnp.tile(jnp.concatenate([cos, cos], axis=1), (1, reps)),
            jnp.tile(jnp.concatenate([-sin, sin], axis=1), (1, reps)))


def _relayout_w_in(w):
    offs = [0]
    for width in (SB_WIDTH, SB_WIDTH, SB_WIDTH, SWA_WIDTH, SWA_KV_WIDTH, SWA_KV_WIDTH,
                  SB_WIDTH, SWA_WIDTH, D_MODEL, D_MODEL):
        offs.append(offs[-1] + width)
    head = lambda base, g: w[:, base + g * HEAD_DIM: base + (g + 1) * HEAD_DIM]
    dup = lambda base: [head(base, g) for g in range(SWA_KV_HEADS) for _ in range(PAIR)]
    parts = [w[:, :offs[4]]] + dup(offs[4]) + dup(offs[5]) + [w[:, offs[6]:]]
    return jnp.concatenate(parts, axis=1).astype(BF16)


def _later_key_selector():
    j = jnp.arange(WIN)
    return (j[:, None] > j[None, :]).astype(BF16)


def kernel(x, meta_tokens, norm_gain, w_in, w_branch_sb, w_branch_swa, w_out, attn_sinks, final_norm_gain):
    b, seq, d = x.shape
    assert d == D_MODEL and norm_gain.shape[0] == 1 and seq % PROJ_TILE_ROWS == 0
    n = b * seq
    x2d = x.reshape(n, d)
    squeeze = lambda a: a.reshape(a.shape[1:])
    w = _relayout_w_in(squeeze(w_in))
    assert w.shape[1] == _C_END
    gain = norm_gain.reshape(1, d)

    cos, sin_signed = _rope_tables(jnp.arange(seq) + N_META)
    tok = _proj_call(x2d, gain, w, cos, sin_signed, PROJ_TILE_ROWS)
    sbq, sbk, sbv, swq, swk, swv, act_sb, act_sw, sig_sb, sig_sw = tok

    meta_tile = jnp.concatenate([jnp.zeros((PAD, d), x.dtype), meta_tokens.astype(x.dtype)], axis=0)
    mcos, msin = _rope_tables(jnp.maximum(jnp.arange(BLOCK) - PAD, 0))
    meta = _proj_call(meta_tile, gain, w, mcos, msin, BLOCK)
    sbk_m, sbv_m, swk_m, swv_m = meta[1], meta[2], meta[4], meta[5]

    r3 = lambda a: a.reshape(b, seq, a.shape[1])
    gated_sb = _sb_call(r3(sbq), r3(sbk), r3(sbv), sbk_m, sbv_m, _later_key_selector(), r3(act_sb))
    gated_sw = _swa_call(squeeze(attn_sinks).astype(F32), r3(swq), r3(swk), r3(swv), swk_m, swv_m, r3(act_sw))

    out = _out_call(x2d, gated_sb.reshape(n, SB_WIDTH), gated_sw.reshape(n, SWA_WIDTH), sig_sb, sig_sw,
                    squeeze(w_branch_sb).astype(BF16), squeeze(w_branch_swa).astype(BF16),
                    squeeze(w_out).astype(BF16), final_norm_gain.reshape(1, d), OUT_TILE_ROWS)
    return out.reshape(b, seq, d)
```

```python
import jax
import jax.numpy as jnp
from jax import lax
from jax.experimental import pallas as pl
from jax.experimental.pallas import tpu as pltpu

F32 = jnp.float32
BF16 = jnp.bfloat16

D_MODEL = 1024
N_META = 16
BLOCK = 128
PAD = BLOCK - N_META
HEAD_DIM = 64
SB_HEADS = 8
SB_WIDTH = SB_HEADS * HEAD_DIM
SWA_Q_HEADS = 16
SWA_KV_HEADS = 2
SWA_GROUP = SWA_Q_HEADS // SWA_KV_HEADS
SWA_WIDTH = SWA_Q_HEADS * HEAD_DIM
SWA_KV_WIDTH = SWA_KV_HEADS * HEAD_DIM
ROPE_THETA = 10000.0
RMS_EPS = 1e-6
LOG2E = 1.4426950408889634
Q_SCALE = HEAD_DIM ** -0.5 * LOG2E

LANES = 128
PAIR = LANES // HEAD_DIM
SB_PAIRS = SB_HEADS // PAIR
KV_DUP_WIDTH = SWA_KV_HEADS * LANES
SWA_STACK = 8
ATTN_ROWS = 256
BF16_PACKED_ROWS = 16

SUB = 64
WIN = 256
LOOKBACK = WIN - SUB

F32_EXP2_UNDERFLOW = -151.0
MASKED_SCORE = -1e30

_C_SBQ = 0
_C_SBK = _C_SBQ + SB_WIDTH
_C_SBV = _C_SBK + SB_WIDTH
_C_SWQ = _C_SBV + SB_WIDTH
_C_SWK = _C_SWQ + SWA_WIDTH
_C_SWV = _C_SWK + SWA_KV_WIDTH
_C_ZSB = _C_SWV + SWA_KV_WIDTH
_C_ZSW = _C_ZSB + SB_WIDTH
_C_GSB = _C_ZSW + SWA_WIDTH
_C_GSW = _C_GSB + D_MODEL
_C_END = _C_GSW + D_MODEL

PROJ_TILE_ROWS = 512
OUT_TILE_ROWS = 1024
VMEM_LIMIT_BYTES = 56 * 1024 * 1024


def _sigmoid(z):
    return 1.0 / (1.0 + jnp.exp(-z))


def _proj_kernel(x_ref, gain_ref, w_ref, cos_ref, sin_ref,
                 sbq_ref, sbk_ref, sbv_ref, swq_ref, swk_ref, swv_ref,
                 asb_ref, asw_ref, ssb_ref, ssw_ref):
    x = x_ref[...]
    ms = jnp.mean(x * x, axis=-1, keepdims=True)
    xn = (x * lax.rsqrt(ms + RMS_EPS) * gain_ref[...]).astype(BF16)

    def mm(c0, width):
        return jnp.dot(xn, w_ref[:, c0:c0 + width], preferred_element_type=F32)

    cos = cos_ref[...]
    sin_signed = sin_ref[...]
    lane = lax.broadcasted_iota(jnp.int32, cos.shape, 1)
    first_half = (lane % HEAD_DIM) < (HEAD_DIM // 2)
    low_half = lane < HEAD_DIM
    zero = jnp.zeros(cos.shape, BF16)

    def rope_tile(tile):
        partner = jnp.where(first_half, pltpu.roll(tile, LANES - HEAD_DIM // 2, 1),
                            pltpu.roll(tile, HEAD_DIM // 2, 1))
        return tile * cos + partner * sin_signed

    def roped(acc, scale):
        tiles = [rope_tile(acc[:, t * LANES:(t + 1) * LANES]) * scale for t in range(acc.shape[1] // LANES)]
        return jnp.concatenate(tiles, axis=1)

    def store_per_head(ref, first_tile, q):
        for t in range(q.shape[1] // LANES):
            tile = q[:, t * LANES:(t + 1) * LANES]
            at = (first_tile + t) * PAIR * LANES
            ref[:, at:at + LANES] = jnp.where(low_half, tile, zero)
            ref[:, at + LANES:at + 2 * LANES] = jnp.where(low_half, zero, tile)

    def store_kv_heads(ref, tile):
        swapped = pltpu.roll(tile, HEAD_DIM, 1)
        ref[:, 0:LANES] = jnp.where(low_half, tile, swapped).astype(BF16)
        ref[:, LANES:2 * LANES] = jnp.where(low_half, swapped, tile).astype(BF16)

    store_per_head(sbq_ref, 0, (mm(_C_SBQ, SB_WIDTH) * Q_SCALE).astype(BF16))
    sbk_ref[...] = mm(_C_SBK, SB_WIDTH).astype(BF16)
    sbv_ref[...] = mm(_C_SBV, SB_WIDTH).astype(BF16)
    half = SWA_WIDTH // 2
    for c in range(2):
        store_per_head(swq_ref, c * (half // LANES), roped(mm(_C_SWQ + c * half, half), Q_SCALE).astype(BF16))
    store_kv_heads(swk_ref, rope_tile(mm(_C_SWK, SWA_KV_WIDTH)))
    store_kv_heads(swv_ref, mm(_C_SWV, SWA_KV_WIDTH))
    z = mm(_C_ZSB, SB_WIDTH)
    asb_ref[...] = (z * _sigmoid(z)).astype(BF16)
    for c in range(2):
        z = mm(_C_ZSW + c * half, half)
        asw_ref[:, c * half:(c + 1) * half] = (z * _sigmoid(z)).astype(BF16)
    for c in range(2):
        ssb_ref[:, c * half:(c + 1) * half] = _sigmoid(mm(_C_GSB + c * half, half)).astype(BF16)
    for c in range(2):
        ssw_ref[:, c * half:(c + 1) * half] = _sigmoid(mm(_C_GSW + c * half, half)).astype(BF16)


def _proj_call(x2d, gain, w, cos, sin_signed, tile_rows):
    n = x2d.shape[0]
    table_tiles = cos.shape[0] // tile_rows
    widths = (PAIR * SB_WIDTH, SB_WIDTH, SB_WIDTH, PAIR * SWA_WIDTH, KV_DUP_WIDTH, KV_DUP_WIDTH,
              SB_WIDTH, SWA_WIDTH, D_MODEL, D_MODEL)
    row_tile = lambda width: pl.BlockSpec((tile_rows, width), lambda i: (i, 0))
    table = pl.BlockSpec((tile_rows, LANES), lambda i: (i % table_tiles, 0))
    return pl.pallas_call(
        _proj_kernel,
        grid=(n // tile_rows,),
        in_specs=[row_tile(D_MODEL),
                  pl.BlockSpec((1, D_MODEL), lambda i: (0, 0)),
                  pl.BlockSpec((D_MODEL, _C_END), lambda i: (0, 0), pipeline_mode=pl.Buffered(1)),
                  table, table],
        out_specs=[row_tile(wd) for wd in widths],
        out_shape=[jax.ShapeDtypeStruct((n, wd), BF16) for wd in widths],
        compiler_params=pltpu.CompilerParams(dimension_semantics=("arbitrary",),
                                             vmem_limit_bytes=VMEM_LIMIT_BYTES),
        name="proj",
    )(x2d, gain, w, cos, sin_signed)


def _log2_beta_terms(z):
    tail = jnp.log2(1.0 + jnp.exp2(-jnp.abs(z)))
    log_beta = jnp.minimum(z, 0.0) - tail
    return log_beta, log_beta - z


def _swa_rows(first, sink_ref, q_ref, kp_ref, ko_ref, vp_ref, vo_ref, km_ref, vm_ref, a_ref, o_ref):
    t = lax.broadcasted_iota(jnp.int32, (BLOCK, 2 * BLOCK), 0)
    c = lax.broadcasted_iota(jnp.int32, (BLOCK, 2 * BLOCK), 1)
    band = (c > t) & (c <= t + BLOCK)
    sink_slot = lax.broadcasted_iota(jnp.int32, (1, 2 * BLOCK), 1) == 0
    low_half = lax.broadcasted_iota(jnp.int32, (BLOCK, LANES), 1) < HEAD_DIM
    ones = jnp.ones((2 * BLOCK, LANES), BF16)
    top_row = lax.broadcasted_iota(jnp.int32, (BF16_PACKED_ROWS, KV_DUP_WIDTH), 0) == 0
    for blk in range(ATTN_ROWS // BLOCK):
        rows = slice(blk * BLOCK, (blk + 1) * BLOCK)
        mask = band
        if blk > 0:
            k_prev = ko_ref[(blk - 1) * BLOCK:blk * BLOCK, :]
            v_prev = vo_ref[(blk - 1) * BLOCK:blk * BLOCK, :]
        elif first:
            k_prev, v_prev = km_ref[...], vm_ref[...]
            mask = band & (c >= PAD)
        else:
            k_prev, v_prev = kp_ref[...], vp_ref[...]
        k_cat = jnp.concatenate([k_prev, ko_ref[rows, :]], axis=0)
        v_head = jnp.where(top_row, jnp.zeros((BF16_PACKED_ROWS, KV_DUP_WIDTH), BF16), v_prev[:BF16_PACKED_ROWS])
        v_cat = jnp.concatenate([v_head, v_prev[BF16_PACKED_ROWS:], vo_ref[rows, :]], axis=0)
        for first_head in range(0, SWA_Q_HEADS, SWA_STACK):
            g = first_head // SWA_GROUP
            kg = k_cat[:, g * LANES:(g + 1) * LANES]
            vg = jnp.concatenate([v_cat[:, g * LANES:(g + 1) * LANES], ones], axis=1)
            q_rows = [q_ref[rows, (first_head + n) * LANES:(first_head + n + 1) * LANES] for n in range(SWA_STACK)]
            s_all = lax.dot_general(jnp.concatenate(q_rows, axis=0), kg, (((1,), (1,)), ((), ())),
                                    preferred_element_type=F32)
            weights = []
            for n in range(SWA_STACK):
                fill = jnp.where(sink_slot, sink_ref[first_head + n] * LOG2E, MASKED_SCORE)
                s = jnp.where(mask, s_all[n * BLOCK:(n + 1) * BLOCK], fill)
                weights.append(jnp.exp2(s - jnp.max(s, axis=1, keepdims=True)).astype(BF16))
            pv = jnp.dot(jnp.concatenate(weights, axis=0), vg, preferred_element_type=F32)
            for pp in range(SWA_STACK // PAIR):
                blk0 = pv[(pp * PAIR) * BLOCK:(pp * PAIR + 1) * BLOCK]
                blk1 = pv[(pp * PAIR + 1) * BLOCK:(pp * PAIR + 2) * BLOCK]
                num = jnp.where(low_half, blk0[:, :LANES], blk1[:, :LANES])
                den = jnp.where(low_half, blk0[:, LANES:], blk1[:, LANES:])
                p = first_head // PAIR + pp
                gate = a_ref[rows, p * LANES:(p + 1) * LANES].astype(F32)
                o_ref[rows, p * LANES:(p + 1) * LANES] = (num * (1.0 / den) * gate).astype(BF16)


def _attn_kernel(sink_ref, q_ref, k_ref, v_ref, km_ref, vm_ref, later_ref, a_ref,
                 swq_ref, swkp_ref, swko_ref, swvp_ref, swvo_ref, swkm_ref, swvm_ref, swa_ref,
                 o_ref, swo_ref):
    i = pl.program_id(1)
    row = lax.broadcasted_iota(jnp.int32, (BLOCK, LANES), 0)
    col = lax.broadcasted_iota(jnp.int32, (BLOCK, LANES), 1)
    sub_row = row & (SUB - 1)
    low_half = lax.broadcasted_iota(jnp.int32, (SUB, LANES), 1) < HEAD_DIM
    later_blk = later_ref[0:BLOCK, 0:BLOCK]
    subs = range(ATTN_ROWS // SUB)
    items = [(s, p) for s in subs for p in range(SB_PAIRS)]

    q_stack = [jnp.concatenate([q_ref[s * SUB:(s + 1) * SUB, (PAIR * p + hh) * LANES:(PAIR * p + hh + 1) * LANES]
                                for hh in range(PAIR)], axis=0) for s, p in items]

    def sliding_window(first):
        _swa_rows(first, sink_ref, swq_ref, swkp_ref, swko_ref, swvp_ref, swvo_ref, swkm_ref, swvm_ref,
                  swa_ref, swo_ref)

    def unstack(res):
        return jnp.where(low_half, res[:SUB], res[SUB:])

    def causal(offset):
        if offset <= -SUB:
            return False
        return None if offset >= BLOCK else col < sub_row + offset

    def prefix(limit):
        if limit <= 0:
            return False
        return None if limit >= BLOCK else col < limit

    def key_block(kb, vb, masks, carry, acc):
        new_carry, new_acc = [], []
        for n, (s, p) in enumerate(items):
            if masks[s] is False:
                new_carry.append(carry[n])
                new_acc.append(acc[n])
                continue
            kp = kb[:, p * LANES:(p + 1) * LANES]
            vp = vb[:, p * LANES:(p + 1) * LANES]
            z = lax.dot_general(q_stack[n], kp, (((1,), (1,)), ((), ())), preferred_element_type=F32)
            log_beta, log_1m = _log2_beta_terms(z)
            if masks[s] is not None:
                log_1m = jnp.where(masks[s], log_1m, 0.0)
            suffix = jnp.dot(log_1m.astype(BF16), later_blk, preferred_element_type=F32)
            w = jnp.exp2(log_beta + suffix + carry[n])
            if masks[s] is not None:
                w = jnp.where(masks[s], w, 0.0)
            new_acc.append(acc[n] + unstack(jnp.dot(w.astype(BF16), vp, preferred_element_type=F32)))
            new_carry.append(carry[n] + jnp.sum(log_1m, axis=1, keepdims=True))
        return new_carry, new_acc

    def least_decayed(carry):
        m = carry[0]
        for c in carry[1:]:
            m = jnp.maximum(m, c)
        return jnp.max(m)

    def older_blocks(peeled, loop_from, carry, acc):
        for blk, masks in peeled:
            s0 = blk * BLOCK if isinstance(blk, int) else pl.multiple_of(blk * BLOCK, BLOCK)
            carry, acc = key_block(k_ref[pl.ds(s0, BLOCK), :], v_ref[pl.ds(s0, BLOCK), :], masks, carry, acc)

        def cond(state):
            j, top, _, _ = state
            return jnp.logical_and(j >= 0, top >= F32_EXP2_UNDERFLOW)

        def body(state):
            j, _, carry, acc = state
            sj = pl.multiple_of(j * BLOCK, BLOCK)
            carry, acc = key_block(k_ref[pl.ds(sj, BLOCK), :], v_ref[pl.ds(sj, BLOCK), :],
                                   tuple(None for _ in subs), carry, acc)
            return j - 1, least_decayed(carry), carry, acc

        _, top, carry, acc = lax.while_loop(cond, body, (loop_from, least_decayed(carry), carry, acc))
        meta_valid = col >= PAD

        def with_meta(carry, acc):
            return key_block(km_ref[...], vm_ref[...], tuple(meta_valid for _ in subs), carry, acc)[1]

        return lax.cond(top >= F32_EXP2_UNDERFLOW, with_meta, lambda carry, acc: acc, carry, acc)

    def first_rows_step():
        sliding_window(True)
        carry = [jnp.zeros((BLOCK, 1), F32) for _ in items]
        acc = [jnp.zeros((SUB, LANES), F32) for _ in items]
        peeled = [(b, tuple(causal(s * SUB - b * BLOCK) for s in subs))
                  for b in reversed(range(ATTN_ROWS // BLOCK))]
        return older_blocks(peeled, jnp.int32(-1), carry, acc)

    def near_window_step():
        visible = col < sub_row + (LOOKBACK - LANES)
        split, log_betas, oldest, starts = [], [], [], []
        for n, (s, p) in enumerate(items):
            start = pl.multiple_of(i * ATTN_ROWS - LOOKBACK + s * SUB, SUB)
            kw = k_ref[pl.ds(start, WIN), p * LANES:(p + 1) * LANES]
            z = lax.dot_general(q_stack[n], kw, (((1,), (1,)), ((), ())), preferred_element_type=F32)
            log_beta, log_1m = _log2_beta_terms(z)
            log_1m = jnp.concatenate([log_1m[:, :LANES], jnp.where(visible, log_1m[:, LANES:], 0.0)], axis=1)
            split.append(log_1m.astype(BF16))
            log_betas.append(log_beta)
            oldest.append(log_1m[:, 0:1])
            starts.append(start)
        suffix_all = jnp.dot(jnp.concatenate(split, axis=0), later_ref[...], preferred_element_type=F32)
        carry, acc = [], []
        for n, (s, p) in enumerate(items):
            suffix = suffix_all[n * BLOCK:(n + 1) * BLOCK]
            w = jnp.exp2(log_betas[n] + suffix)
            w = jnp.concatenate([w[:, :LANES], jnp.where(visible, w[:, LANES:], 0.0)], axis=1)
            vw = v_ref[pl.ds(starts[n], WIN), p * LANES:(p + 1) * LANES]
            acc.append(unstack(jnp.dot(w.astype(BF16), vw, preferred_element_type=F32)))
            carry.append(suffix[:, 0:1] + oldest[n])
        sliding_window(False)

        def keys_before_windows(carry, acc):
            first = i * (ATTN_ROWS // BLOCK)
            peeled, back = [], 1
            while back * BLOCK - LOOKBACK < BLOCK:
                peeled.append((first - back, tuple(prefix(s * SUB - LOOKBACK + back * BLOCK) for s in subs)))
                back += 1
            return older_blocks(peeled, first - back, carry, acc)

        return lax.cond(least_decayed(carry) >= F32_EXP2_UNDERFLOW, keys_before_windows,
                        lambda carry, acc: acc, carry, acc)

    acc = lax.cond(i * ATTN_ROWS >= LOOKBACK, near_window_step, first_rows_step)
    for n, (s, p) in enumerate(items):
        gate = a_ref[s * SUB:(s + 1) * SUB, p * LANES:(p + 1) * LANES].astype(F32)
        o_ref[s * SUB:(s + 1) * SUB, p * LANES:(p + 1) * LANES] = (acc[n] * gate).astype(BF16)


def _attn_call(sinks, sbq, sbk, sbv, sbk_meta, sbv_meta, later, act_sb,
               swq, swk, swv, swk_meta, swv_meta, act_sw):
    b, seq, sb_width = sbk.shape
    sw_width, kvw = act_sw.shape[2], swk.shape[2]
    blocks_per_step = ATTN_ROWS // BLOCK
    rows = lambda width: pl.BlockSpec((None, ATTN_ROWS, width), lambda bi, i: (bi, i, 0))
    whole = pl.BlockSpec((None, seq, sb_width), lambda bi, i: (bi, 0, 0), pipeline_mode=pl.Buffered(1))
    const = lambda shape: pl.BlockSpec(shape, lambda bi, i: (0, 0))
    kv_prev = pl.BlockSpec((None, BLOCK, kvw), lambda bi, i: (bi, jnp.maximum(i * blocks_per_step - 1, 0), 0))
    return pl.pallas_call(
        _attn_kernel,
        grid=(b, seq // ATTN_ROWS),
        in_specs=[pl.BlockSpec(memory_space=pltpu.SMEM),
                  rows(sbq.shape[2]), whole, whole, const((BLOCK, sb_width)), const((BLOCK, sb_width)),
                  const(later.shape), rows(sb_width),
                  rows(swq.shape[2]), kv_prev, rows(kvw), kv_prev, rows(kvw),
                  const((BLOCK, kvw)), const((BLOCK, kvw)), rows(sw_width)],
        out_specs=[rows(sb_width), rows(sw_width)],
        out_shape=[jax.ShapeDtypeStruct((b, seq, sb_width), BF16), jax.ShapeDtypeStruct((b, seq, sw_width), BF16)],
        compiler_params=pltpu.CompilerParams(dimension_semantics=("arbitrary", "arbitrary"),
                                             vmem_limit_bytes=VMEM_LIMIT_BYTES),
        name="attn",
    )(sinks, sbq, sbk, sbv, sbk_meta, sbv_meta, later, act_sb,
      swq, swk, swk, swv, swv, swk_meta, swv_meta, act_sw)


def _out_kernel(x_ref, asb_ref, asw_ref, ssb_ref, ssw_ref, wsb_ref, wsw_ref, wout_ref, gain_ref, o_ref):
    y_sb = jnp.dot(asb_ref[...], wsb_ref[...], preferred_element_type=F32)
    y_sw = jnp.dot(asw_ref[...], wsw_ref[...], preferred_element_type=F32)
    merged = ssb_ref[...].astype(F32) * y_sb + ssw_ref[...].astype(F32) * y_sw
    h = x_ref[...] + jnp.dot(merged.astype(BF16), wout_ref[...], preferred_element_type=F32)
    ms = jnp.mean(h * h, axis=-1, keepdims=True)
    o_ref[...] = h * lax.rsqrt(ms + RMS_EPS) * gain_ref[...]


def _out_call(x2d, act_sb, act_sw, sig_sb, sig_sw, w_sb, w_sw, w_out, gain, tile_rows):
    n = x2d.shape[0]
    row_tile = lambda width: pl.BlockSpec((tile_rows, width), lambda i: (i, 0))
    const = lambda shape: pl.BlockSpec(shape, lambda i: (0, 0))
    return pl.pallas_call(
        _out_kernel,
        grid=(n // tile_rows,),
        in_specs=[row_tile(D_MODEL), row_tile(SB_WIDTH), row_tile(SWA_WIDTH), row_tile(D_MODEL),
                  row_tile(D_MODEL), const(w_sb.shape), const(w_sw.shape), const(w_out.shape),
                  const((1, D_MODEL))],
        out_specs=row_tile(D_MODEL),
        out_shape=jax.ShapeDtypeStruct((n, D_MODEL), F32),
        compiler_params=pltpu.CompilerParams(dimension_semantics=("arbitrary",),
                                             vmem_limit_bytes=VMEM_LIMIT_BYTES),
        name="out_proj",
    )(x2d, act_sb, act_sw, sig_sb, sig_sw, w_sb, w_sw, w_out, gain)


def _rope_tables(pos):
    half = HEAD_DIM // 2
    inv = ROPE_THETA ** (-jnp.arange(half, dtype=F32) / half)
    ang = pos.astype(F32)[:, None] * inv[None, :]
    cos, sin = jnp.cos(ang), jnp.sin(ang)
    reps = LANES // HEAD_DIM
    return (jnp.tile(jnp.concatenate([cos, cos], axis=1), (1, reps)),
            jnp.tile(jnp.concatenate([-sin, sin], axis=1), (1, reps)))


def _later_key_selector():
    j = jnp.arange(WIN)
    return (j[:, None] > j[None, :]).astype(BF16)


def kernel(x, meta_tokens, norm_gain, w_in, w_branch_sb, w_branch_swa, w_out, attn_sinks, final_norm_gain):
    b, seq, d = x.shape
    assert d == D_MODEL and norm_gain.shape[0] == 1 and seq % OUT_TILE_ROWS == 0 and w_in.shape[2] == _C_END
    n = b * seq
    x2d = x.reshape(n, d)
    squeeze = lambda a: a.reshape(a.shape[1:])
    w = squeeze(w_in).astype(BF16)
    gain = norm_gain.reshape(1, d)

    cos, sin_signed = _rope_tables(jnp.arange(seq) + N_META)
    tok = _proj_call(x2d, gain, w, cos, sin_signed, PROJ_TILE_ROWS)
    sbq, sbk, sbv, swq, swk, swv, act_sb, act_sw, sig_sb, sig_sw = tok

    meta_tile = jnp.concatenate([jnp.zeros((PAD, d), x.dtype), meta_tokens.astype(x.dtype)], axis=0)
    mcos, msin = _rope_tables(jnp.maximum(jnp.arange(BLOCK) - PAD, 0))
    meta = _proj_call(meta_tile, gain, w, mcos, msin, BLOCK)
    sbk_m, sbv_m, swk_m, swv_m = meta[1], meta[2], meta[4], meta[5]

    r3 = lambda a: a.reshape(b, seq, a.shape[1])
    gated_sb, gated_sw = _attn_call(squeeze(attn_sinks).astype(F32), r3(sbq), r3(sbk), r3(sbv), sbk_m, sbv_m,
                                    _later_key_selector(), r3(act_sb),
                                    r3(swq), r3(swk), r3(swv), swk_m, swv_m, r3(act_sw))

    out = _out_call(x2d, gated_sb.reshape(n, SB_WIDTH), gated_sw.reshape(n, SWA_WIDTH), sig_sb, sig_sw,
                    squeeze(w_branch_sb).astype(BF16), squeeze(w_branch_swa).astype(BF16),
                    squeeze(w_out).astype(BF16), final_norm_gain.reshape(1, d), OUT_TILE_ROWS)
    return out.reshape(b, seq, d)
```

```python
import jax
import jax.numpy as jnp
from jax import lax
from jax.experimental import pallas as pl
from jax.experimental.pallas import tpu as pltpu

F32 = jnp.float32
BF16 = jnp.bfloat16

D_MODEL = 1024
N_META = 16
BLOCK = 128
PAD = BLOCK - N_META
HEAD_DIM = 64
SB_HEADS = 8
SB_WIDTH = SB_HEADS * HEAD_DIM
SWA_Q_HEADS = 16
SWA_KV_HEADS = 2
SWA_GROUP = SWA_Q_HEADS // SWA_KV_HEADS
SWA_WIDTH = SWA_Q_HEADS * HEAD_DIM
SWA_KV_WIDTH = SWA_KV_HEADS * HEAD_DIM
ROPE_THETA = 10000.0
RMS_EPS = 1e-6
LOG2E = 1.4426950408889634
Q_SCALE = HEAD_DIM ** -0.5 * LOG2E

LANES = 128
PAIR = LANES // HEAD_DIM
SB_PAIRS = SB_HEADS // PAIR
KV_DUP_WIDTH = SWA_KV_HEADS * LANES
SWA_STACK = 8
ATTN_ROWS = 512
BF16_PACKED_ROWS = 16

SUB = 64
WIN = 256
LOOKBACK = WIN - SUB

F32_EXP2_UNDERFLOW = -151.0
MASKED_SCORE = -1e30

_C_SBQ = 0
_C_SBK = _C_SBQ + SB_WIDTH
_C_SBV = _C_SBK + SB_WIDTH
_C_SWQ = _C_SBV + SB_WIDTH
_C_SWK = _C_SWQ + SWA_WIDTH
_C_SWV = _C_SWK + SWA_KV_WIDTH
_C_ZSB = _C_SWV + SWA_KV_WIDTH
_C_ZSW = _C_ZSB + SB_WIDTH
_C_GSB = _C_ZSW + SWA_WIDTH
_C_GSW = _C_GSB + D_MODEL
_C_END = _C_GSW + D_MODEL

PROJ_TILE_ROWS = 512
OUT_TILE_ROWS = 1024
OUT_CHUNK_ROWS = 256
VMEM_LIMIT_BYTES = 56 * 1024 * 1024


def _sigmoid(z):
    return 1.0 / (1.0 + jnp.exp(-z))


def _proj_kernel(x_ref, gain_ref, w_ref, cos_ref, sin_ref,
                 sbq_ref, sbk_ref, sbv_ref, swq_ref, swk_ref, swv_ref,
                 asb_ref, asw_ref, ssb_ref, ssw_ref):
    x = x_ref[...]
    ms = jnp.mean(x * x, axis=-1, keepdims=True)
    xn = (x * lax.rsqrt(ms + RMS_EPS) * gain_ref[...]).astype(BF16)

    def mm(c0, width):
        return jnp.dot(xn, w_ref[:, c0:c0 + width], preferred_element_type=F32)

    cos = cos_ref[...]
    sin_signed = sin_ref[...]
    lane = lax.broadcasted_iota(jnp.int32, cos.shape, 1)
    first_half = (lane % HEAD_DIM) < (HEAD_DIM // 2)
    low_half = lane < HEAD_DIM
    zero = jnp.zeros(cos.shape, BF16)

    def rope_tile(tile):
        partner = jnp.where(first_half, pltpu.roll(tile, LANES - HEAD_DIM // 2, 1),
                            pltpu.roll(tile, HEAD_DIM // 2, 1))
        return tile * cos + partner * sin_signed

    def roped(acc, scale):
        tiles = [rope_tile(acc[:, t * LANES:(t + 1) * LANES]) * scale for t in range(acc.shape[1] // LANES)]
        return jnp.concatenate(tiles, axis=1)

    def store_per_head(ref, first_tile, q):
        for t in range(q.shape[1] // LANES):
            tile = q[:, t * LANES:(t + 1) * LANES]
            at = (first_tile + t) * PAIR * LANES
            ref[:, at:at + LANES] = jnp.where(low_half, tile, zero)
            ref[:, at + LANES:at + 2 * LANES] = jnp.where(low_half, zero, tile)

    def store_kv_heads(ref, tile):
        swapped = pltpu.roll(tile, HEAD_DIM, 1)
        ref[:, 0:LANES] = jnp.where(low_half, tile, swapped).astype(BF16)
        ref[:, LANES:2 * LANES] = jnp.where(low_half, swapped, tile).astype(BF16)

    half = SWA_WIDTH // 2
    for c in range(2):
        z = mm(_C_ZSW + c * half, half)
        asw_ref[:, c * half:(c + 1) * half] = (z * _sigmoid(z)).astype(BF16)
    for c in range(2):
        ssb_ref[:, c * half:(c + 1) * half] = _sigmoid(mm(_C_GSB + c * half, half)).astype(BF16)
    for c in range(2):
        ssw_ref[:, c * half:(c + 1) * half] = _sigmoid(mm(_C_GSW + c * half, half)).astype(BF16)
    z = mm(_C_ZSB, SB_WIDTH)
    asb_ref[...] = (z * _sigmoid(z)).astype(BF16)
    for c in range(2):
        store_per_head(swq_ref, c * (half // LANES), roped(mm(_C_SWQ + c * half, half), Q_SCALE).astype(BF16))
    store_kv_heads(swk_ref, rope_tile(mm(_C_SWK, SWA_KV_WIDTH)))
    store_kv_heads(swv_ref, mm(_C_SWV, SWA_KV_WIDTH))
    store_per_head(sbq_ref, 0, (mm(_C_SBQ, SB_WIDTH) * Q_SCALE).astype(BF16))
    sbk_ref[...] = mm(_C_SBK, SB_WIDTH).astype(BF16)
    sbv_ref[...] = mm(_C_SBV, SB_WIDTH).astype(BF16)


def _proj_call(x2d, gain, w, cos, sin_signed, tile_rows):
    n = x2d.shape[0]
    table_tiles = cos.shape[0] // tile_rows
    widths = (PAIR * SB_WIDTH, SB_WIDTH, SB_WIDTH, PAIR * SWA_WIDTH, KV_DUP_WIDTH, KV_DUP_WIDTH,
              SB_WIDTH, SWA_WIDTH, D_MODEL, D_MODEL)
    row_tile = lambda width: pl.BlockSpec((tile_rows, width), lambda i: (i, 0))
    table = pl.BlockSpec((tile_rows, LANES), lambda i: (i % table_tiles, 0))
    return pl.pallas_call(
        _proj_kernel,
        grid=(n // tile_rows,),
        in_specs=[row_tile(D_MODEL),
                  pl.BlockSpec((1, D_MODEL), lambda i: (0, 0)),
                  pl.BlockSpec((D_MODEL, _C_END), lambda i: (0, 0), pipeline_mode=pl.Buffered(1)),
                  table, table],
        out_specs=[row_tile(wd) for wd in widths],
        out_shape=[jax.ShapeDtypeStruct((n, wd), BF16) for wd in widths],
        compiler_params=pltpu.CompilerParams(dimension_semantics=("arbitrary",),
                                             vmem_limit_bytes=VMEM_LIMIT_BYTES),
        name="proj",
    )(x2d, gain, w, cos, sin_signed)


def _log2_beta_terms(z):
    tail = jnp.log2(1.0 + jnp.exp2(-jnp.abs(z)))
    log_beta = jnp.minimum(z, 0.0) - tail
    return log_beta, log_beta - z


def _swa_rows(first, sink_ref, q_ref, kp_ref, ko_ref, vp_ref, vo_ref, km_ref, vm_ref, a_ref, o_ref):
    t = lax.broadcasted_iota(jnp.int32, (BLOCK, 2 * BLOCK), 0)
    c = lax.broadcasted_iota(jnp.int32, (BLOCK, 2 * BLOCK), 1)
    band = (c > t) & (c <= t + BLOCK)
    sink_slot = lax.broadcasted_iota(jnp.int32, (1, 2 * BLOCK), 1) == 0
    low_half = lax.broadcasted_iota(jnp.int32, (BLOCK, LANES), 1) < HEAD_DIM
    ones = jnp.ones((2 * BLOCK, LANES), BF16)
    top_row = lax.broadcasted_iota(jnp.int32, (BF16_PACKED_ROWS, KV_DUP_WIDTH), 0) == 0
    for blk in range(ATTN_ROWS // BLOCK):
        rows = slice(blk * BLOCK, (blk + 1) * BLOCK)
        mask = band
        if blk > 0:
            k_prev = ko_ref[(blk - 1) * BLOCK:blk * BLOCK, :]
            v_prev = vo_ref[(blk - 1) * BLOCK:blk * BLOCK, :]
        elif first:
            k_prev, v_prev = km_ref[...], vm_ref[...]
            mask = band & (c >= PAD)
        else:
            k_prev, v_prev = kp_ref[...], vp_ref[...]
        k_cat = jnp.concatenate([k_prev, ko_ref[rows, :]], axis=0)
        v_head = jnp.where(top_row, jnp.zeros((BF16_PACKED_ROWS, KV_DUP_WIDTH), BF16), v_prev[:BF16_PACKED_ROWS])
        v_cat = jnp.concatenate([v_head, v_prev[BF16_PACKED_ROWS:], vo_ref[rows, :]], axis=0)
        for first_head in range(0, SWA_Q_HEADS, SWA_STACK):
            g = first_head // SWA_GROUP
            kg = k_cat[:, g * LANES:(g + 1) * LANES]
            vg = jnp.concatenate([v_cat[:, g * LANES:(g + 1) * LANES], ones], axis=1)
            q_rows = [q_ref[rows, (first_head + n) * LANES:(first_head + n + 1) * LANES] for n in range(SWA_STACK)]
            s_all = lax.dot_general(jnp.concatenate(q_rows, axis=0), kg, (((1,), (1,)), ((), ())),
                                    preferred_element_type=F32)
            weights = []
            for n in range(SWA_STACK):
                fill = jnp.where(sink_slot, sink_ref[first_head + n] * LOG2E, MASKED_SCORE)
                s = jnp.where(mask, s_all[n * BLOCK:(n + 1) * BLOCK], fill)
                weights.append(jnp.exp2(s - jnp.max(s, axis=1, keepdims=True)).astype(BF16))
            pv = jnp.dot(jnp.concatenate(weights, axis=0), vg, preferred_element_type=F32)
            for pp in range(SWA_STACK // PAIR):
                blk0 = pv[(pp * PAIR) * BLOCK:(pp * PAIR + 1) * BLOCK]
                blk1 = pv[(pp * PAIR + 1) * BLOCK:(pp * PAIR + 2) * BLOCK]
                num = jnp.where(low_half, blk0[:, :LANES], blk1[:, :LANES])
                den = jnp.where(low_half, blk0[:, LANES:], blk1[:, LANES:])
                p = first_head // PAIR + pp
                gate = a_ref[rows, p * LANES:(p + 1) * LANES].astype(F32)
                o_ref[rows, p * LANES:(p + 1) * LANES] = (num * (1.0 / den) * gate).astype(BF16)


def _attn_kernel(sink_ref, q_ref, k_ref, v_ref, km_ref, vm_ref, later_ref, a_ref,
                 swq_ref, swkp_ref, swko_ref, swvp_ref, swvo_ref, swkm_ref, swvm_ref, swa_ref,
                 o_ref, swo_ref):
    i = pl.program_id(1)
    row = lax.broadcasted_iota(jnp.int32, (BLOCK, LANES), 0)
    col = lax.broadcasted_iota(jnp.int32, (BLOCK, LANES), 1)
    sub_row = row & (SUB - 1)
    low_half = lax.broadcasted_iota(jnp.int32, (SUB, LANES), 1) < HEAD_DIM
    later_blk = later_ref[0:BLOCK, 0:BLOCK]
    subs = range(ATTN_ROWS // SUB)
    items = [(s, p) for s in subs for p in range(SB_PAIRS)]

    q_stack = [jnp.concatenate([q_ref[s * SUB:(s + 1) * SUB, (PAIR * p + hh) * LANES:(PAIR * p + hh + 1) * LANES]
                                for hh in range(PAIR)], axis=0) for s, p in items]

    def sliding_window(first):
        _swa_rows(first, sink_ref, swq_ref, swkp_ref, swko_ref, swvp_ref, swvo_ref, swkm_ref, swvm_ref,
                  swa_ref, swo_ref)

    def unstack(res):
        return jnp.where(low_half, res[:SUB], res[SUB:])

    def causal(offset):
        if offset <= -SUB:
            return False
        return None if offset >= BLOCK else col < sub_row + offset

    def prefix(limit):
        if limit <= 0:
            return False
        return None if limit >= BLOCK else col < limit

    def key_block(kb, vb, masks, carry, acc):
        new_carry, new_acc = [], []
        for n, (s, p) in enumerate(items):
            if masks[s] is False:
                new_carry.append(carry[n])
                new_acc.append(acc[n])
                continue
            kp = kb[:, p * LANES:(p + 1) * LANES]
            vp = vb[:, p * LANES:(p + 1) * LANES]
            z = lax.dot_general(q_stack[n], kp, (((1,), (1,)), ((), ())), preferred_element_type=F32)
            log_beta, log_1m = _log2_beta_terms(z)
            if masks[s] is not None:
                log_1m = jnp.where(masks[s], log_1m, 0.0)
            suffix = jnp.dot(log_1m.astype(BF16), later_blk, preferred_element_type=F32)
            w = jnp.exp2(log_beta + suffix + carry[n])
            if masks[s] is not None:
                w = jnp.where(masks[s], w, 0.0)
            new_acc.append(acc[n] + unstack(jnp.dot(w.astype(BF16), vp, preferred_element_type=F32)))
            new_carry.append(carry[n] + jnp.sum(log_1m, axis=1, keepdims=True))
        return new_carry, new_acc

    def least_decayed(carry):
        m = carry[0]
        for c in carry[1:]:
            m = jnp.maximum(m, c)
        return jnp.max(m)

    def older_blocks(peeled, loop_from, carry, acc):
        for blk, masks in peeled:
            s0 = blk * BLOCK if isinstance(blk, int) else pl.multiple_of(blk * BLOCK, BLOCK)
            carry, acc = key_block(k_ref[pl.ds(s0, BLOCK), :], v_ref[pl.ds(s0, BLOCK), :], masks, carry, acc)

        def cond(state):
            j, top, _, _ = state
            return jnp.logical_and(j >= 0, top >= F32_EXP2_UNDERFLOW)

        def body(state):
            j, _, carry, acc = state
            sj = pl.multiple_of(j * BLOCK, BLOCK)
            carry, acc = key_block(k_ref[pl.ds(sj, BLOCK), :], v_ref[pl.ds(sj, BLOCK), :],
                                   tuple(None for _ in subs), carry, acc)
            return j - 1, least_decayed(carry), carry, acc

        _, top, carry, acc = lax.while_loop(cond, body, (loop_from, least_decayed(carry), carry, acc))
        meta_valid = col >= PAD

        def with_meta(carry, acc):
            return key_block(km_ref[...], vm_ref[...], tuple(meta_valid for _ in subs), carry, acc)[1]

        return lax.cond(top >= F32_EXP2_UNDERFLOW, with_meta, lambda carry, acc: acc, carry, acc)

    def first_rows_step():
        sliding_window(True)
        carry = [jnp.zeros((BLOCK, 1), F32) for _ in items]
        acc = [jnp.zeros((SUB, LANES), F32) for _ in items]
        peeled = [(b, tuple(causal(s * SUB - b * BLOCK) for s in subs))
                  for b in reversed(range(ATTN_ROWS // BLOCK))]
        return older_blocks(peeled, jnp.int32(-1), carry, acc)

    def near_window_step():
        visible = col < sub_row + (LOOKBACK - LANES)
        split, log_betas, oldest, starts = [], [], [], []
        for n, (s, p) in enumerate(items):
            start = pl.multiple_of(i * ATTN_ROWS - LOOKBACK + s * SUB, SUB)
            kw = k_ref[pl.ds(start, WIN), p * LANES:(p + 1) * LANES]
            z = lax.dot_general(q_stack[n], kw, (((1,), (1,)), ((), ())), preferred_element_type=F32)
            log_beta, log_1m = _log2_beta_terms(z)
            log_1m = jnp.concatenate([log_1m[:, :LANES], jnp.where(visible, log_1m[:, LANES:], 0.0)], axis=1)
            split.append(log_1m.astype(BF16))
            log_betas.append(log_beta)
            oldest.append(log_1m[:, 0:1])
            starts.append(start)
        suffix_all = jnp.dot(jnp.concatenate(split, axis=0), later_ref[...], preferred_element_type=F32)
        carry, acc = [], []
        for n, (s, p) in enumerate(items):
            suffix = suffix_all[n * BLOCK:(n + 1) * BLOCK]
            w = jnp.exp2(log_betas[n] + suffix)
            w = jnp.concatenate([w[:, :LANES], jnp.where(visible, w[:, LANES:], 0.0)], axis=1)
            vw = v_ref[pl.ds(starts[n], WIN), p * LANES:(p + 1) * LANES]
            acc.append(unstack(jnp.dot(w.astype(BF16), vw, preferred_element_type=F32)))
            carry.append(suffix[:, 0:1] + oldest[n])
        sliding_window(False)

        def keys_before_windows(carry, acc):
            first = i * (ATTN_ROWS // BLOCK)
            peeled, back = [], 1
            while back * BLOCK - LOOKBACK < BLOCK:
                peeled.append((first - back, tuple(prefix(s * SUB - LOOKBACK + back * BLOCK) for s in subs)))
                back += 1
            return older_blocks(peeled, first - back, carry, acc)

        return lax.cond(least_decayed(carry) >= F32_EXP2_UNDERFLOW, keys_before_windows,
                        lambda carry, acc: acc, carry, acc)

    acc = lax.cond(i * ATTN_ROWS >= LOOKBACK, near_window_step, first_rows_step)
    for n, (s, p) in enumerate(items):
        gate = a_ref[s * SUB:(s + 1) * SUB, p * LANES:(p + 1) * LANES].astype(F32)
        o_ref[s * SUB:(s + 1) * SUB, p * LANES:(p + 1) * LANES] = (acc[n] * gate).astype(BF16)


def _attn_call(sinks, sbq, sbk, sbv, sbk_meta, sbv_meta, later, act_sb,
               swq, swk, swv, swk_meta, swv_meta, act_sw):
    b, seq, sb_width = sbk.shape
    sw_width, kvw = act_sw.shape[2], swk.shape[2]
    blocks_per_step = ATTN_ROWS // BLOCK
    rows = lambda width: pl.BlockSpec((None, ATTN_ROWS, width), lambda bi, i: (bi, i, 0))
    whole = pl.BlockSpec((None, seq, sb_width), lambda bi, i: (bi, 0, 0), pipeline_mode=pl.Buffered(1))
    const = lambda shape: pl.BlockSpec(shape, lambda bi, i: (0, 0))
    kv_prev = pl.BlockSpec((None, BLOCK, kvw), lambda bi, i: (bi, jnp.maximum(i * blocks_per_step - 1, 0), 0))
    return pl.pallas_call(
        _attn_kernel,
        grid=(b, seq // ATTN_ROWS),
        in_specs=[pl.BlockSpec(memory_space=pltpu.SMEM),
                  rows(sbq.shape[2]), whole, whole, const((BLOCK, sb_width)), const((BLOCK, sb_width)),
                  const(later.shape), rows(sb_width),
                  rows(swq.shape[2]), kv_prev, rows(kvw), kv_prev, rows(kvw),
                  const((BLOCK, kvw)), const((BLOCK, kvw)), rows(sw_width)],
        out_specs=[rows(sb_width), rows(sw_width)],
        out_shape=[jax.ShapeDtypeStruct((b, seq, sb_width), BF16), jax.ShapeDtypeStruct((b, seq, sw_width), BF16)],
        compiler_params=pltpu.CompilerParams(dimension_semantics=("arbitrary", "arbitrary"),
                                             vmem_limit_bytes=VMEM_LIMIT_BYTES),
        name="attn",
    )(sinks, sbq, sbk, sbv, sbk_meta, sbv_meta, later, act_sb,
      swq, swk, swk, swv, swv, swk_meta, swv_meta, act_sw)


def _out_kernel(x_ref, asb_ref, asw_ref, ssb_ref, ssw_ref, wsb_ref, wsw_ref, wout_ref, gain_ref, o_ref):
    for r in range(0, x_ref.shape[0], OUT_CHUNK_ROWS):
        rows = slice(r, r + OUT_CHUNK_ROWS)
        y_sb = jnp.dot(asb_ref[rows, :], wsb_ref[...], preferred_element_type=F32)
        y_sw = jnp.dot(asw_ref[rows, :], wsw_ref[...], preferred_element_type=F32)
        merged = ssb_ref[rows, :].astype(F32) * y_sb + ssw_ref[rows, :].astype(F32) * y_sw
        h = x_ref[rows, :] + jnp.dot(merged.astype(BF16), wout_ref[...], preferred_element_type=F32)
        ms = jnp.mean(h * h, axis=-1, keepdims=True)
        o_ref[rows, :] = h * lax.rsqrt(ms + RMS_EPS) * gain_ref[...]


def _out_call(x2d, act_sb, act_sw, sig_sb, sig_sw, w_sb, w_sw, w_out, gain, tile_rows):
    n = x2d.shape[0]
    row_tile = lambda width: pl.BlockSpec((tile_rows, width), lambda i: (i, 0))
    const = lambda shape: pl.BlockSpec(shape, lambda i: (0, 0))
    return pl.pallas_call(
        _out_kernel,
        grid=(n // tile_rows,),
        in_specs=[row_tile(D_MODEL), row_tile(SB_WIDTH), row_tile(SWA_WIDTH), row_tile(D_MODEL),
                  row_tile(D_MODEL), const(w_sb.shape), const(w_sw.shape), const(w_out.shape),
                  const((1, D_MODEL))],
        out_specs=row_tile(D_MODEL),
        out_shape=jax.ShapeDtypeStruct((n, D_MODEL), F32),
        compiler_params=pltpu.CompilerParams(dimension_semantics=("arbitrary",),
                                             vmem_limit_bytes=VMEM_LIMIT_BYTES),
        name="out_proj",
    )(x2d, act_sb, act_sw, sig_sb, sig_sw, w_sb, w_sw, w_out, gain)


def _rope_tables(pos):
    half = HEAD_DIM // 2
    inv = ROPE_THETA ** (-jnp.arange(half, dtype=F32) / half)
    ang = pos.astype(F32)[:, None] * inv[None, :]
    cos, sin = jnp.cos(ang), jnp.sin(ang)
    reps = LANES // HEAD_DIM
    return (jnp.tile(jnp.concatenate([cos, cos], axis=1), (1, reps)),
            jnp.tile(jnp.concatenate([-sin, sin], axis=1), (1, reps)))


def _later_key_selector():
    j = jnp.arange(WIN)
    return (j[:, None] > j[None, :]).astype(BF16)


def kernel(x, meta_tokens, norm_gain, w_in, w_branch_sb, w_branch_swa, w_out, attn_sinks, final_norm_gain):
    b, seq, d = x.shape
    assert d == D_MODEL and norm_gain.shape[0] == 1 and seq % OUT_TILE_ROWS == 0 and w_in.shape[2] == _C_END
    n = b * seq
    x2d = x.reshape(n, d)
    squeeze = lambda a: a.reshape(a.shape[1:])
    w = squeeze(w_in).astype(BF16)
    gain = norm_gain.reshape(1, d)

    cos, sin_signed = _rope_tables(jnp.arange(seq) + N_META)
    tok = _proj_call(x2d, gain, w, cos, sin_signed, PROJ_TILE_ROWS)
    sbq, sbk, sbv, swq, swk, swv, act_sb, act_sw, sig_sb, sig_sw = tok

    meta_tile = jnp.concatenate([jnp.zeros((PAD, d), x.dtype), meta_tokens.astype(x.dtype)], axis=0)
    mcos, msin = _rope_tables(jnp.maximum(jnp.arange(BLOCK) - PAD, 0))
    meta = _proj_call(meta_tile, gain, w, mcos, msin, BLOCK)
    sbk_m, sbv_m, swk_m, swv_m = meta[1], meta[2], meta[4], meta[5]

    r3 = lambda a: a.reshape(b, seq, a.shape[1])
    gated_sb, gated_sw = _attn_call(squeeze(attn_sinks).astype(F32), r3(sbq), r3(sbk), r3(sbv), sbk_m, sbv_m,
                                    _later_key_selector(), r3(act_sb),
                                    r3(swq), r3(swk), r3(swv), swk_m, swv_m, r3(act_sw))

    out = _out_call(x2d, gated_sb.reshape(n, SB_WIDTH), gated_sw.reshape(n, SWA_WIDTH), sig_sb, sig_sw,
                    squeeze(w_branch_sb).astype(BF16), squeeze(w_branch_swa).astype(BF16),
                    squeeze(w_out).astype(BF16), final_norm_gain.reshape(1, d), OUT_TILE_ROWS)
    return out.reshape(b, seq, d)
```

```python
import jax
import jax.numpy as jnp
import numpy as np
from jax import lax
from jax.experimental import pallas as pl
from jax.experimental.pallas import tpu as pltpu

F32 = jnp.float32
BF16 = jnp.bfloat16

D_MODEL = 1024
N_META = 16
BLOCK = 128
PAD = BLOCK - N_META
HEAD_DIM = 64
SB_HEADS = 8
SB_WIDTH = SB_HEADS * HEAD_DIM
SWA_Q_HEADS = 16
SWA_KV_HEADS = 2
SWA_GROUP = SWA_Q_HEADS // SWA_KV_HEADS
SWA_WIDTH = SWA_Q_HEADS * HEAD_DIM
SWA_KV_WIDTH = SWA_KV_HEADS * HEAD_DIM
ROPE_THETA = 10000.0
RMS_EPS = 1e-6
LOG2E = 1.4426950408889634
Q_SCALE = HEAD_DIM ** -0.5 * LOG2E

LANES = 128
PAIR = LANES // HEAD_DIM
SB_PAIRS = SB_HEADS // PAIR
KV_DUP_WIDTH = SWA_KV_HEADS * LANES
SWA_STACK = 8
ATTN_ROWS = 256
BF16_PACKED_ROWS = 16

SUB = 64
WIN = 256
LOOKBACK = WIN - SUB

F32_EXP2_UNDERFLOW = -151.0
MASKED_SCORE = -1e30

_C_SBQ = 0
_C_SBK = _C_SBQ + SB_WIDTH
_C_SBV = _C_SBK + SB_WIDTH
_C_SWQ = _C_SBV + SB_WIDTH
_C_SWK = _C_SWQ + SWA_WIDTH
_C_SWV = _C_SWK + SWA_KV_WIDTH
_C_ZSB = _C_SWV + SWA_KV_WIDTH
_C_ZSW = _C_ZSB + SB_WIDTH
_C_GSB = _C_ZSW + SWA_WIDTH
_C_GSW = _C_GSB + D_MODEL
_C_END = _C_GSW + D_MODEL

PROJ_TILE_ROWS = 512
PROJ_CHUNK_COLS = 512
OUT_TILE_ROWS = 1024
OUT_CHUNK_ROWS = 256
VMEM_LIMIT_BYTES = 56 * 1024 * 1024


def _sigmoid(z):
    return 1.0 / (1.0 + jnp.exp(-z))


def _proj_kernel(x_ref, gain_ref, w_ref, cos_ref, sin_ref,
                 sbq_ref, sbk_ref, sbv_ref, swq_ref, swk_ref, swv_ref,
                 asb_ref, asw_ref, ssb_ref, ssw_ref):
    x = x_ref[...]
    ms = jnp.mean(x * x, axis=-1, keepdims=True)
    xn = (x * lax.rsqrt(ms + RMS_EPS) * gain_ref[...]).astype(BF16)

    def mm(c0, width):
        return jnp.dot(xn, w_ref[:, c0:c0 + width], preferred_element_type=F32)

    cos = cos_ref[...]
    sin_signed = sin_ref[...]
    lane = lax.broadcasted_iota(jnp.int32, cos.shape, 1)
    first_half = (lane % HEAD_DIM) < (HEAD_DIM // 2)
    low_half = lane < HEAD_DIM
    zero = jnp.zeros(cos.shape, BF16)

    def rope_tile(tile):
        partner = jnp.where(first_half, pltpu.roll(tile, LANES - HEAD_DIM // 2, 1),
                            pltpu.roll(tile, HEAD_DIM // 2, 1))
        return tile * cos + partner * sin_signed

    def roped(acc, scale):
        tiles = [rope_tile(acc[:, t * LANES:(t + 1) * LANES]) * scale for t in range(acc.shape[1] // LANES)]
        return jnp.concatenate(tiles, axis=1)

    def store_per_head(ref, first_tile, q):
        for t in range(q.shape[1] // LANES):
            tile = q[:, t * LANES:(t + 1) * LANES]
            at = (first_tile + t) * PAIR * LANES
            ref[:, at:at + LANES] = jnp.where(low_half, tile, zero)
            ref[:, at + LANES:at + 2 * LANES] = jnp.where(low_half, zero, tile)

    def store_kv_heads(ref, tile):
        swapped = pltpu.roll(tile, HEAD_DIM, 1)
        ref[:, 0:LANES] = jnp.where(low_half, tile, swapped).astype(BF16)
        ref[:, LANES:2 * LANES] = jnp.where(low_half, swapped, tile).astype(BF16)

    ch = PROJ_CHUNK_COLS

    def chunks(width):
        return range(0, width, ch)

    for c in chunks(SWA_WIDTH):
        z = mm(_C_ZSW + c, ch)
        asw_ref[:, c:c + ch] = (z * _sigmoid(z)).astype(BF16)
    for c in chunks(D_MODEL):
        ssb_ref[:, c:c + ch] = _sigmoid(mm(_C_GSB + c, ch)).astype(BF16)
    for c in chunks(D_MODEL):
        ssw_ref[:, c:c + ch] = _sigmoid(mm(_C_GSW + c, ch)).astype(BF16)
    for c in chunks(SB_WIDTH):
        z = mm(_C_ZSB + c, ch)
        asb_ref[:, c:c + ch] = (z * _sigmoid(z)).astype(BF16)
    for c in chunks(SWA_WIDTH):
        store_per_head(swq_ref, c // LANES, roped(mm(_C_SWQ + c, ch), Q_SCALE).astype(BF16))
    store_kv_heads(swk_ref, rope_tile(mm(_C_SWK, SWA_KV_WIDTH)))
    store_kv_heads(swv_ref, mm(_C_SWV, SWA_KV_WIDTH))
    for c in chunks(SB_WIDTH):
        store_per_head(sbq_ref, c // LANES, (mm(_C_SBQ + c, ch) * Q_SCALE).astype(BF16))
    for c in chunks(SB_WIDTH):
        sbk_ref[:, c:c + ch] = mm(_C_SBK + c, ch).astype(BF16)
    for c in chunks(SB_WIDTH):
        sbv_ref[:, c:c + ch] = mm(_C_SBV + c, ch).astype(BF16)


def _proj_call(x2d, gain, w, cos, sin_signed, tile_rows):
    n = x2d.shape[0]
    table_tiles = cos.shape[0] // tile_rows
    widths = (PAIR * SB_WIDTH, SB_WIDTH, SB_WIDTH, PAIR * SWA_WIDTH, KV_DUP_WIDTH, KV_DUP_WIDTH,
              SB_WIDTH, SWA_WIDTH, D_MODEL, D_MODEL)
    row_tile = lambda width: pl.BlockSpec((tile_rows, width), lambda i: (i, 0))
    table = pl.BlockSpec((tile_rows, LANES), lambda i: (i % table_tiles, 0))
    return pl.pallas_call(
        _proj_kernel,
        grid=(n // tile_rows,),
        in_specs=[row_tile(D_MODEL),
                  pl.BlockSpec((1, D_MODEL), lambda i: (0, 0)),
                  pl.BlockSpec((D_MODEL, _C_END), lambda i: (0, 0), pipeline_mode=pl.Buffered(1)),
                  table, table],
        out_specs=[row_tile(wd) for wd in widths],
        out_shape=[jax.ShapeDtypeStruct((n, wd), BF16) for wd in widths],
        compiler_params=pltpu.CompilerParams(dimension_semantics=("arbitrary",),
                                             vmem_limit_bytes=VMEM_LIMIT_BYTES),
        name="proj",
    )(x2d, gain, w, cos, sin_signed)


def _log2_beta_terms(z):
    tail = jnp.log2(1.0 + jnp.exp2(-jnp.abs(z)))
    log_beta = jnp.minimum(z, 0.0) - tail
    return log_beta, log_beta - z


def _swa_rows(first, sink_ref, q_ref, kp_ref, ko_ref, vp_ref, vo_ref, km_ref, vm_ref, a_ref, o_ref):
    t = lax.broadcasted_iota(jnp.int32, (BLOCK, 2 * BLOCK), 0)
    c = lax.broadcasted_iota(jnp.int32, (BLOCK, 2 * BLOCK), 1)
    band = (c > t) & (c <= t + BLOCK)
    sink_slot = lax.broadcasted_iota(jnp.int32, (1, 2 * BLOCK), 1) == 0
    low_half = lax.broadcasted_iota(jnp.int32, (BLOCK, LANES), 1) < HEAD_DIM
    ones = jnp.ones((2 * BLOCK, LANES), BF16)
    top_row = lax.broadcasted_iota(jnp.int32, (BF16_PACKED_ROWS, KV_DUP_WIDTH), 0) == 0
    for blk in range(ATTN_ROWS // BLOCK):
        rows = slice(blk * BLOCK, (blk + 1) * BLOCK)
        mask = band
        if blk > 0:
            k_prev = ko_ref[(blk - 1) * BLOCK:blk * BLOCK, :]
            v_prev = vo_ref[(blk - 1) * BLOCK:blk * BLOCK, :]
        elif first:
            k_prev, v_prev = km_ref[...], vm_ref[...]
            mask = band & (c >= PAD)
        else:
            k_prev, v_prev = kp_ref[...], vp_ref[...]
        k_cat = jnp.concatenate([k_prev, ko_ref[rows, :]], axis=0)
        v_head = jnp.where(top_row, jnp.zeros((BF16_PACKED_ROWS, KV_DUP_WIDTH), BF16), v_prev[:BF16_PACKED_ROWS])
        v_cat = jnp.concatenate([v_head, v_prev[BF16_PACKED_ROWS:], vo_ref[rows, :]], axis=0)
        for first_head in range(0, SWA_Q_HEADS, SWA_STACK):
            g = first_head // SWA_GROUP
            kg = k_cat[:, g * LANES:(g + 1) * LANES]
            vg = jnp.concatenate([v_cat[:, g * LANES:(g + 1) * LANES], ones], axis=1)
            q_rows = [q_ref[rows, (first_head + n) * LANES:(first_head + n + 1) * LANES] for n in range(SWA_STACK)]
            s_all = lax.dot_general(jnp.concatenate(q_rows, axis=0), kg, (((1,), (1,)), ((), ())),
                                    preferred_element_type=F32)
            weights = []
            for n in range(SWA_STACK):
                fill = jnp.where(sink_slot, sink_ref[first_head + n] * LOG2E, MASKED_SCORE)
                s = jnp.where(mask, s_all[n * BLOCK:(n + 1) * BLOCK], fill)
                weights.append(jnp.exp2(s - jnp.max(s, axis=1, keepdims=True)).astype(BF16))
            pv = jnp.dot(jnp.concatenate(weights, axis=0), vg, preferred_element_type=F32)
            for pp in range(SWA_STACK // PAIR):
                blk0 = pv[(pp * PAIR) * BLOCK:(pp * PAIR + 1) * BLOCK]
                blk1 = pv[(pp * PAIR + 1) * BLOCK:(pp * PAIR + 2) * BLOCK]
                num = jnp.where(low_half, blk0[:, :LANES], blk1[:, :LANES])
                den = jnp.where(low_half, blk0[:, LANES:], blk1[:, LANES:])
                p = first_head // PAIR + pp
                gate = a_ref[rows, p * LANES:(p + 1) * LANES].astype(F32)
                o_ref[rows, p * LANES:(p + 1) * LANES] = (num * (1.0 / den) * gate).astype(BF16)


def _attn_kernel(sink_ref, q_ref, k_ref, v_ref, km_ref, vm_ref, later_ref, a_ref,
                 swq_ref, swkp_ref, swko_ref, swvp_ref, swvo_ref, swkm_ref, swvm_ref, swa_ref,
                 o_ref, swo_ref):
    i = pl.program_id(1)
    row = lax.broadcasted_iota(jnp.int32, (BLOCK, LANES), 0)
    col = lax.broadcasted_iota(jnp.int32, (BLOCK, LANES), 1)
    sub_row = row & (SUB - 1)
    low_half = lax.broadcasted_iota(jnp.int32, (SUB, LANES), 1) < HEAD_DIM
    later_blk = later_ref[0:BLOCK, 0:BLOCK]
    subs = range(ATTN_ROWS // SUB)
    items = [(s, p) for s in subs for p in range(SB_PAIRS)]

    q_stack = [jnp.concatenate([q_ref[s * SUB:(s + 1) * SUB, (PAIR * p + hh) * LANES:(PAIR * p + hh + 1) * LANES]
                                for hh in range(PAIR)], axis=0) for s, p in items]

    def sliding_window(first):
        _swa_rows(first, sink_ref, swq_ref, swkp_ref, swko_ref, swvp_ref, swvo_ref, swkm_ref, swvm_ref,
                  swa_ref, swo_ref)

    def unstack(res):
        return jnp.where(low_half, res[:SUB], res[SUB:])

    def causal(offset):
        if offset <= -SUB:
            return False
        return None if offset >= BLOCK else col < sub_row + offset

    def prefix(limit):
        if limit <= 0:
            return False
        return None if limit >= BLOCK else col < limit

    def key_block(kb, vb, masks, carry, acc):
        new_carry, new_acc = [], []
        for n, (s, p) in enumerate(items):
            if masks[s] is False:
                new_carry.append(carry[n])
                new_acc.append(acc[n])
                continue
            kp = kb[:, p * LANES:(p + 1) * LANES]
            vp = vb[:, p * LANES:(p + 1) * LANES]
            z = lax.dot_general(q_stack[n], kp, (((1,), (1,)), ((), ())), preferred_element_type=F32)
            log_beta, log_1m = _log2_beta_terms(z)
            if masks[s] is not None:
                log_1m = jnp.where(masks[s], log_1m, 0.0)
            suffix = jnp.dot(log_1m.astype(BF16), later_blk, preferred_element_type=F32)
            w = jnp.exp2(log_beta + suffix + carry[n])
            if masks[s] is not None:
                w = jnp.where(masks[s], w, 0.0)
            new_acc.append(acc[n] + unstack(jnp.dot(w.astype(BF16), vp, preferred_element_type=F32)))
            new_carry.append(carry[n] + jnp.sum(log_1m, axis=1, keepdims=True))
        return new_carry, new_acc

    def least_decayed(carry):
        m = carry[0]
        for c in carry[1:]:
            m = jnp.maximum(m, c)
        return jnp.max(m)

    def older_blocks(peeled, loop_from, carry, acc):
        for blk, masks in peeled:
            s0 = blk * BLOCK if isinstance(blk, int) else pl.multiple_of(blk * BLOCK, BLOCK)
            carry, acc = key_block(k_ref[pl.ds(s0, BLOCK), :], v_ref[pl.ds(s0, BLOCK), :], masks, carry, acc)

        def cond(state):
            j, top, _, _ = state
            return jnp.logical_and(j >= 0, top >= F32_EXP2_UNDERFLOW)

        def body(state):
            j, _, carry, acc = state
            sj = pl.multiple_of(j * BLOCK, BLOCK)
            carry, acc = key_block(k_ref[pl.ds(sj, BLOCK), :], v_ref[pl.ds(sj, BLOCK), :],
                                   tuple(None for _ in subs), carry, acc)
            return j - 1, least_decayed(carry), carry, acc

        _, top, carry, acc = lax.while_loop(cond, body, (loop_from, least_decayed(carry), carry, acc))
        meta_valid = col >= PAD

        def with_meta(carry, acc):
            return key_block(km_ref[...], vm_ref[...], tuple(meta_valid for _ in subs), carry, acc)[1]

        return lax.cond(top >= F32_EXP2_UNDERFLOW, with_meta, lambda carry, acc: acc, carry, acc)

    def first_rows_step():
        sliding_window(True)
        carry = [jnp.zeros((BLOCK, 1), F32) for _ in items]
        acc = [jnp.zeros((SUB, LANES), F32) for _ in items]
        peeled = [(b, tuple(causal(s * SUB - b * BLOCK) for s in subs))
                  for b in reversed(range(ATTN_ROWS // BLOCK))]
        return older_blocks(peeled, jnp.int32(-1), carry, acc)

    def near_window_step():
        visible = col < sub_row + (LOOKBACK - LANES)
        split, log_betas, starts = [], [], []
        for n, (s, p) in enumerate(items):
            start = pl.multiple_of(i * ATTN_ROWS - LOOKBACK + s * SUB, SUB)
            kw = k_ref[pl.ds(start, WIN), p * LANES:(p + 1) * LANES]
            z = lax.dot_general(q_stack[n], kw, (((1,), (1,)), ((), ())), preferred_element_type=F32)
            log_beta, log_1m = _log2_beta_terms(z)
            log_1m = jnp.concatenate([log_1m[:, :LANES], jnp.where(visible, log_1m[:, LANES:], 0.0)], axis=1)
            split.append(log_1m.astype(BF16))
            log_betas.append(log_beta)
            starts.append(start)
        suffix_all = jnp.dot(jnp.concatenate(split, axis=0), later_ref[...], preferred_element_type=F32)
        carry, acc = [], []
        for n, (s, p) in enumerate(items):
            suffix = suffix_all[n * BLOCK:(n + 1) * BLOCK]
            w = jnp.exp2(log_betas[n] + suffix)
            w = jnp.concatenate([w[:, :LANES], jnp.where(visible, w[:, LANES:], 0.0)], axis=1)
            vw = v_ref[pl.ds(starts[n], WIN), p * LANES:(p + 1) * LANES]
            acc.append(unstack(jnp.dot(w.astype(BF16), vw, preferred_element_type=F32)))
            carry.append(suffix[:, WIN - 1:WIN])
        sliding_window(False)

        def keys_before_windows(carry, acc):
            first = i * (ATTN_ROWS // BLOCK)
            peeled, back = [], 1
            while back * BLOCK - LOOKBACK < BLOCK:
                peeled.append((first - back, tuple(prefix(s * SUB - LOOKBACK + back * BLOCK) for s in subs)))
                back += 1
            return older_blocks(peeled, first - back, carry, acc)

        return lax.cond(least_decayed(carry) >= F32_EXP2_UNDERFLOW, keys_before_windows,
                        lambda carry, acc: acc, carry, acc)

    acc = lax.cond(i * ATTN_ROWS >= LOOKBACK, near_window_step, first_rows_step)
    for n, (s, p) in enumerate(items):
        gate = a_ref[s * SUB:(s + 1) * SUB, p * LANES:(p + 1) * LANES].astype(F32)
        o_ref[s * SUB:(s + 1) * SUB, p * LANES:(p + 1) * LANES] = (acc[n] * gate).astype(BF16)


def _attn_call(sinks, sbq, sbk, sbv, sbk_meta, sbv_meta, later, act_sb,
               swq, swk, swv, swk_meta, swv_meta, act_sw):
    b, seq, sb_width = sbk.shape
    sw_width, kvw = act_sw.shape[2], swk.shape[2]
    blocks_per_step = ATTN_ROWS // BLOCK
    rows = lambda width: pl.BlockSpec((None, ATTN_ROWS, width), lambda bi, i: (bi, i, 0))
    whole = pl.BlockSpec((None, seq, sb_width), lambda bi, i: (bi, 0, 0))
    const = lambda shape: pl.BlockSpec(shape, lambda bi, i: (0, 0))
    kv_prev = pl.BlockSpec((None, BLOCK, kvw), lambda bi, i: (bi, jnp.maximum(i * blocks_per_step - 1, 0), 0))
    return pl.pallas_call(
        _attn_kernel,
        grid=(b, seq // ATTN_ROWS),
        in_specs=[pl.BlockSpec(memory_space=pltpu.SMEM),
                  rows(sbq.shape[2]), whole, whole, const((BLOCK, sb_width)), const((BLOCK, sb_width)),
                  const(later.shape), rows(sb_width),
                  rows(swq.shape[2]), kv_prev, rows(kvw), kv_prev, rows(kvw),
                  const((BLOCK, kvw)), const((BLOCK, kvw)), rows(sw_width)],
        out_specs=[rows(sb_width), rows(sw_width)],
        out_shape=[jax.ShapeDtypeStruct((b, seq, sb_width), BF16), jax.ShapeDtypeStruct((b, seq, sw_width), BF16)],
        compiler_params=pltpu.CompilerParams(dimension_semantics=("arbitrary", "arbitrary"),
                                             vmem_limit_bytes=VMEM_LIMIT_BYTES),
        name="attn",
    )(sinks, sbq, sbk, sbv, sbk_meta, sbv_meta, later, act_sb,
      swq, swk, swk, swv, swv, swk_meta, swv_meta, act_sw)


def _out_kernel(x_ref, asb_ref, asw_ref, ssb_ref, ssw_ref, wsb_ref, wsw_ref, wout_ref, gain_ref, o_ref):
    for r in range(0, x_ref.shape[0], OUT_CHUNK_ROWS):
        rows = slice(r, r + OUT_CHUNK_ROWS)
        y_sb = jnp.dot(asb_ref[rows, :], wsb_ref[...], preferred_element_type=F32)
        y_sw = jnp.dot(asw_ref[rows, :], wsw_ref[...], preferred_element_type=F32)
        merged = ssb_ref[rows, :].astype(F32) * y_sb + ssw_ref[rows, :].astype(F32) * y_sw
        h = x_ref[rows, :] + jnp.dot(merged.astype(BF16), wout_ref[...], preferred_element_type=F32)
        ms = jnp.mean(h * h, axis=-1, keepdims=True)
        o_ref[rows, :] = h * lax.rsqrt(ms + RMS_EPS) * gain_ref[...]


def _out_call(x2d, act_sb, act_sw, sig_sb, sig_sw, w_sb, w_sw, w_out, gain, tile_rows):
    n = x2d.shape[0]
    row_tile = lambda width: pl.BlockSpec((tile_rows, width), lambda i: (i, 0))
    const = lambda shape: pl.BlockSpec(shape, lambda i: (0, 0))
    return pl.pallas_call(
        _out_kernel,
        grid=(n // tile_rows,),
        in_specs=[row_tile(D_MODEL), row_tile(SB_WIDTH), row_tile(SWA_WIDTH), row_tile(D_MODEL),
                  row_tile(D_MODEL), const(w_sb.shape), const(w_sw.shape), const(w_out.shape),
                  const((1, D_MODEL))],
        out_specs=row_tile(D_MODEL),
        out_shape=jax.ShapeDtypeStruct((n, D_MODEL), F32),
        compiler_params=pltpu.CompilerParams(dimension_semantics=("arbitrary",),
                                             vmem_limit_bytes=VMEM_LIMIT_BYTES),
        name="out_proj",
    )(x2d, act_sb, act_sw, sig_sb, sig_sw, w_sb, w_sw, w_out, gain)


def _rope_tables(pos):
    half = HEAD_DIM // 2
    inv = (np.float32(ROPE_THETA) ** (-np.arange(half, dtype=np.float32) / np.float32(half))).astype(np.float32)
    ang = pos.astype(np.float32)[:, None] * inv[None, :]
    cos, sin = np.cos(ang), np.sin(ang)
    reps = LANES // HEAD_DIM
    return (jnp.asarray(np.tile(np.concatenate([cos, cos], axis=1), (1, reps)), F32),
            jnp.asarray(np.tile(np.concatenate([-sin, sin], axis=1), (1, reps)), F32))


def _later_key_selector():
    j = np.arange(WIN)
    later = j[:, None] > j[None, :]
    later[:, WIN - 1] = True
    return jnp.asarray(later, BF16)


def kernel(x, meta_tokens, norm_gain, w_in, w_branch_sb, w_branch_swa, w_out, attn_sinks, final_norm_gain):
    b, seq, d = x.shape
    assert d == D_MODEL and norm_gain.shape[0] == 1 and seq % OUT_TILE_ROWS == 0 and w_in.shape[2] == _C_END
    n = b * seq
    x2d = x.reshape(n, d)
    squeeze = lambda a: a.reshape(a.shape[1:])
    w = squeeze(w_in).astype(BF16)
    gain = norm_gain.reshape(1, d)

    cos, sin_signed = _rope_tables(np.arange(seq) + N_META)
    tok = _proj_call(x2d, gain, w, cos, sin_signed, PROJ_TILE_ROWS)
    sbq, sbk, sbv, swq, swk, swv, act_sb, act_sw, sig_sb, sig_sw = tok

    meta_tile = jnp.concatenate([jnp.zeros((PAD, d), x.dtype), meta_tokens.astype(x.dtype)], axis=0)
    mcos, msin = _rope_tables(np.maximum(np.arange(BLOCK) - PAD, 0))
    meta = _proj_call(meta_tile, gain, w, mcos, msin, BLOCK)
    sbk_m, sbv_m, swk_m, swv_m = meta[1], meta[2], meta[4], meta[5]

    r3 = lambda a: a.reshape(b, seq, a.shape[1])
    gated_sb, gated_sw = _attn_call(squeeze(attn_sinks).astype(F32), r3(sbq), r3(sbk), r3(sbv), sbk_m, sbv_m,
                                    _later_key_selector(), r3(act_sb),
                                    r3(swq), r3(swk), r3(swv), swk_m, swv_m, r3(act_sw))

    out = _out_call(x2d, gated_sb.reshape(n, SB_WIDTH), gated_sw.reshape(n, SWA_WIDTH), sig_sb, sig_sw,
                    squeeze(w_branch_sb).astype(BF16), squeeze(w_branch_swa).astype(BF16),
                    squeeze(w_out).astype(BF16), final_norm_gain.reshape(1, d), OUT_TILE_ROWS)
    return out.reshape(b, seq, d)
```

```python
import jax
import jax.numpy as jnp
import numpy as np
from jax import lax
from jax.experimental import pallas as pl
from jax.experimental.pallas import tpu as pltpu

F32 = jnp.float32
BF16 = jnp.bfloat16

D_MODEL = 1024
N_META = 16
BLOCK = 128
PAD = BLOCK - N_META
HEAD_DIM = 64
SB_HEADS = 8
SB_WIDTH = SB_HEADS * HEAD_DIM
SWA_Q_HEADS = 16
SWA_KV_HEADS = 2
SWA_GROUP = SWA_Q_HEADS // SWA_KV_HEADS
SWA_WIDTH = SWA_Q_HEADS * HEAD_DIM
SWA_KV_WIDTH = SWA_KV_HEADS * HEAD_DIM
ROPE_THETA = 10000.0
RMS_EPS = 1e-6
LOG2E = 1.4426950408889634
Q_SCALE = HEAD_DIM ** -0.5 * LOG2E

LANES = 128
PAIR = LANES // HEAD_DIM
SB_PAIRS = SB_HEADS // PAIR
KV_DUP_WIDTH = SWA_KV_HEADS * LANES
SWA_STACK = 8
ATTN_ROWS = 256
BF16_PACKED_ROWS = 16

SUB = 64
WIN = 256
LOOKBACK = WIN - SUB

F32_EXP2_UNDERFLOW = -151.0
MASKED_SCORE = -1e30

_C_SBQ = 0
_C_SBK = _C_SBQ + SB_WIDTH
_C_SBV = _C_SBK + SB_WIDTH
_C_SWQ = _C_SBV + SB_WIDTH
_C_SWK = _C_SWQ + SWA_WIDTH
_C_SWV = _C_SWK + SWA_KV_WIDTH
_C_ZSB = _C_SWV + SWA_KV_WIDTH
_C_ZSW = _C_ZSB + SB_WIDTH
_C_GSB = _C_ZSW + SWA_WIDTH
_C_GSW = _C_GSB + D_MODEL
_C_END = _C_GSW + D_MODEL

PROJ_TILE_ROWS = 512
PROJ_CHUNK_COLS = 512
OUT_TILE_ROWS = 1024
OUT_CHUNK_ROWS = 256
VMEM_LIMIT_BYTES = 56 * 1024 * 1024


def _sigmoid(z):
    return 1.0 / (1.0 + jnp.exp(-z))


def _proj_kernel(x_ref, gain_ref, w_ref, cos_ref, sin_ref,
                 sbq_ref, sbk_ref, sbv_ref, swq_ref, swkv_ref,
                 asb_ref, asw_ref, ssb_ref, ssw_ref):
    x = x_ref[...]
    ms = jnp.mean(x * x, axis=-1, keepdims=True)
    xn = (x * lax.rsqrt(ms + RMS_EPS) * gain_ref[...]).astype(BF16)

    def mm(c0, width):
        return jnp.dot(xn, w_ref[:, c0:c0 + width], preferred_element_type=F32)

    cos = cos_ref[...]
    sin_signed = sin_ref[...]
    lane = lax.broadcasted_iota(jnp.int32, cos.shape, 1)
    first_half = (lane % HEAD_DIM) < (HEAD_DIM // 2)
    low_half = lane < HEAD_DIM
    zero = jnp.zeros(cos.shape, BF16)

    def rope_tile(tile):
        partner = jnp.where(first_half, pltpu.roll(tile, LANES - HEAD_DIM // 2, 1),
                            pltpu.roll(tile, HEAD_DIM // 2, 1))
        return tile * cos + partner * sin_signed

    def roped(acc, scale):
        tiles = [rope_tile(acc[:, t * LANES:(t + 1) * LANES]) * scale for t in range(acc.shape[1] // LANES)]
        return jnp.concatenate(tiles, axis=1)

    def store_per_head(ref, first_tile, q):
        for t in range(q.shape[1] // LANES):
            tile = q[:, t * LANES:(t + 1) * LANES]
            at = (first_tile + t) * PAIR * LANES
            ref[:, at:at + LANES] = jnp.where(low_half, tile, zero)
            ref[:, at + LANES:at + 2 * LANES] = jnp.where(low_half, zero, tile)

    def store_kv_heads(ref, at, tile):
        swapped = pltpu.roll(tile, HEAD_DIM, 1)
        ref[:, at:at + LANES] = jnp.where(low_half, tile, swapped).astype(BF16)
        ref[:, at + LANES:at + 2 * LANES] = jnp.where(low_half, swapped, tile).astype(BF16)

    ch = PROJ_CHUNK_COLS

    def chunks(width):
        return range(0, width, ch)

    for c in chunks(SWA_WIDTH):
        z = mm(_C_ZSW + c, ch)
        asw_ref[:, c:c + ch] = (z * _sigmoid(z)).astype(BF16)
    for c in chunks(D_MODEL):
        ssb_ref[:, c:c + ch] = _sigmoid(mm(_C_GSB + c, ch)).astype(BF16)
    for c in chunks(D_MODEL):
        ssw_ref[:, c:c + ch] = _sigmoid(mm(_C_GSW + c, ch)).astype(BF16)
    for c in chunks(SB_WIDTH):
        z = mm(_C_ZSB + c, ch)
        asb_ref[:, c:c + ch] = (z * _sigmoid(z)).astype(BF16)
    for c in chunks(SWA_WIDTH):
        store_per_head(swq_ref, c // LANES, roped(mm(_C_SWQ + c, ch), Q_SCALE).astype(BF16))
    store_kv_heads(swkv_ref, 0, rope_tile(mm(_C_SWK, SWA_KV_WIDTH)))
    store_kv_heads(swkv_ref, KV_DUP_WIDTH, mm(_C_SWV, SWA_KV_WIDTH))
    for c in chunks(SB_WIDTH):
        store_per_head(sbq_ref, c // LANES, (mm(_C_SBQ + c, ch) * Q_SCALE).astype(BF16))
    for c in chunks(SB_WIDTH):
        sbk_ref[:, c:c + ch] = mm(_C_SBK + c, ch).astype(BF16)
    for c in chunks(SB_WIDTH):
        sbv_ref[:, c:c + ch] = mm(_C_SBV + c, ch).astype(BF16)


def _proj_call(x2d, gain, w, cos, sin_signed, tile_rows):
    n = x2d.shape[0]
    table_tiles = cos.shape[0] // tile_rows
    widths = (PAIR * SB_WIDTH, SB_WIDTH, SB_WIDTH, PAIR * SWA_WIDTH, 2 * KV_DUP_WIDTH,
              SB_WIDTH, SWA_WIDTH, D_MODEL, D_MODEL)
    row_tile = lambda width: pl.BlockSpec((tile_rows, width), lambda i: (i, 0))
    table = pl.BlockSpec((tile_rows, LANES), lambda i: (i % table_tiles, 0))
    return pl.pallas_call(
        _proj_kernel,
        grid=(n // tile_rows,),
        in_specs=[row_tile(D_MODEL),
                  pl.BlockSpec((1, D_MODEL), lambda i: (0, 0)),
                  pl.BlockSpec((D_MODEL, _C_END), lambda i: (0, 0), pipeline_mode=pl.Buffered(1)),
                  table, table],
        out_specs=[row_tile(wd) for wd in widths],
        out_shape=[jax.ShapeDtypeStruct((n, wd), BF16) for wd in widths],
        compiler_params=pltpu.CompilerParams(dimension_semantics=("arbitrary",),
                                             vmem_limit_bytes=VMEM_LIMIT_BYTES),
        name="proj",
    )(x2d, gain, w, cos, sin_signed)


def _log2_beta_terms(z):
    tail = jnp.log2(1.0 + jnp.exp2(-jnp.abs(z)))
    log_beta = jnp.minimum(z, 0.0) - tail
    return log_beta, log_beta - z


def _swa_rows(first, sink_ref, q_ref, kp_ref, ko_ref, vp_ref, vo_ref, km_ref, vm_ref, a_ref, o_ref):
    t = lax.broadcasted_iota(jnp.int32, (BLOCK, 2 * BLOCK), 0)
    c = lax.broadcasted_iota(jnp.int32, (BLOCK, 2 * BLOCK), 1)
    band = (c > t) & (c <= t + BLOCK)
    sink_slot = lax.broadcasted_iota(jnp.int32, (1, 2 * BLOCK), 1) == 0
    low_half = lax.broadcasted_iota(jnp.int32, (BLOCK, LANES), 1) < HEAD_DIM
    ones = jnp.ones((2 * BLOCK, LANES), BF16)
    top_row = lax.broadcasted_iota(jnp.int32, (BF16_PACKED_ROWS, KV_DUP_WIDTH), 0) == 0
    for blk in range(ATTN_ROWS // BLOCK):
        rows = slice(blk * BLOCK, (blk + 1) * BLOCK)
        mask = band
        if blk > 0:
            k_prev = ko_ref[(blk - 1) * BLOCK:blk * BLOCK, :]
            v_prev = vo_ref[(blk - 1) * BLOCK:blk * BLOCK, :]
        elif first:
            k_prev, v_prev = km_ref[...], vm_ref[...]
            mask = band & (c >= PAD)
        else:
            k_prev, v_prev = kp_ref[...], vp_ref[...]
        k_cat = jnp.concatenate([k_prev, ko_ref[rows, :]], axis=0)
        v_head = jnp.where(top_row, jnp.zeros((BF16_PACKED_ROWS, KV_DUP_WIDTH), BF16), v_prev[:BF16_PACKED_ROWS])
        v_cat = jnp.concatenate([v_head, v_prev[BF16_PACKED_ROWS:], vo_ref[rows, :]], axis=0)
        for first_head in range(0, SWA_Q_HEADS, SWA_STACK):
            g = first_head // SWA_GROUP
            kg = k_cat[:, g * LANES:(g + 1) * LANES]
            vg = jnp.concatenate([v_cat[:, g * LANES:(g + 1) * LANES], ones], axis=1)
            q_rows = [q_ref[rows, (first_head + n) * LANES:(first_head + n + 1) * LANES] for n in range(SWA_STACK)]
            s_all = lax.dot_general(jnp.concatenate(q_rows, axis=0), kg, (((1,), (1,)), ((), ())),
                                    preferred_element_type=F32)
            weights = []
            for n in range(SWA_STACK):
                fill = jnp.where(sink_slot, sink_ref[first_head + n] * LOG2E, MASKED_SCORE)
                s = jnp.where(mask, s_all[n * BLOCK:(n + 1) * BLOCK], fill)
                weights.append(jnp.exp2(s - jnp.max(s, axis=1, keepdims=True)).astype(BF16))
            pv = jnp.dot(jnp.concatenate(weights, axis=0), vg, preferred_element_type=F32)
            for pp in range(SWA_STACK // PAIR):
                blk0 = pv[(pp * PAIR) * BLOCK:(pp * PAIR + 1) * BLOCK]
                blk1 = pv[(pp * PAIR + 1) * BLOCK:(pp * PAIR + 2) * BLOCK]
                num = jnp.where(low_half, blk0[:, :LANES], blk1[:, :LANES])
                den = jnp.where(low_half, blk0[:, LANES:], blk1[:, LANES:])
                p = first_head // PAIR + pp
                gate = a_ref[rows, p * LANES:(p + 1) * LANES].astype(F32)
                o_ref[rows, p * LANES:(p + 1) * LANES] = (num * (1.0 / den) * gate).astype(BF16)


def _attn_kernel(sink_ref, q_ref, k_ref, v_ref, km_ref, vm_ref, later_ref, a_ref,
                 swq_ref, swkvp_ref, swkvo_ref, swkvm_ref, swa_ref,
                 o_ref, swo_ref):
    i = pl.program_id(1)
    row = lax.broadcasted_iota(jnp.int32, (BLOCK, LANES), 0)
    col = lax.broadcasted_iota(jnp.int32, (BLOCK, LANES), 1)
    sub_row = row & (SUB - 1)
    low_half = lax.broadcasted_iota(jnp.int32, (SUB, LANES), 1) < HEAD_DIM
    later_blk = later_ref[0:BLOCK, 0:BLOCK]
    subs = range(ATTN_ROWS // SUB)
    items = [(s, p) for s in subs for p in range(SB_PAIRS)]

    q_stack = [jnp.concatenate([q_ref[s * SUB:(s + 1) * SUB, (PAIR * p + hh) * LANES:(PAIR * p + hh + 1) * LANES]
                                for hh in range(PAIR)], axis=0) for s, p in items]

    def sliding_window(first):
        keys = lambda ref: ref.at[:, 0:KV_DUP_WIDTH]
        values = lambda ref: ref.at[:, KV_DUP_WIDTH:2 * KV_DUP_WIDTH]
        _swa_rows(first, sink_ref, swq_ref, keys(swkvp_ref), keys(swkvo_ref), values(swkvp_ref), values(swkvo_ref),
                  keys(swkvm_ref), values(swkvm_ref), swa_ref, swo_ref)

    def unstack(res):
        return jnp.where(low_half, res[:SUB], res[SUB:])

    def causal(offset):
        if offset <= -SUB:
            return False
        return None if offset >= BLOCK else col < sub_row + offset

    def prefix(limit):
        if limit <= 0:
            return False
        return None if limit >= BLOCK else col < limit

    def key_block(kb, vb, masks, carry, acc):
        new_carry, new_acc = [], []
        for n, (s, p) in enumerate(items):
            if masks[s] is False:
                new_carry.append(carry[n])
                new_acc.append(acc[n])
                continue
            kp = kb[:, p * LANES:(p + 1) * LANES]
            vp = vb[:, p * LANES:(p + 1) * LANES]
            z = lax.dot_general(q_stack[n], kp, (((1,), (1,)), ((), ())), preferred_element_type=F32)
            log_beta, log_1m = _log2_beta_terms(z)
            if masks[s] is not None:
                log_1m = jnp.where(masks[s], log_1m, 0.0)
            suffix = jnp.dot(log_1m.astype(BF16), later_blk, preferred_element_type=F32)
            w = jnp.exp2(log_beta + suffix + carry[n])
            if masks[s] is not None:
                w = jnp.where(masks[s], w, 0.0)
            new_acc.append(acc[n] + unstack(jnp.dot(w.astype(BF16), vp, preferred_element_type=F32)))
            new_carry.append(carry[n] + jnp.sum(log_1m, axis=1, keepdims=True))
        return new_carry, new_acc

    def least_decayed(carry):
        m = carry[0]
        for c in carry[1:]:
            m = jnp.maximum(m, c)
        return jnp.max(m)

    def older_blocks(peeled, loop_from, carry, acc):
        for blk, masks in peeled:
            s0 = blk * BLOCK if isinstance(blk, int) else pl.multiple_of(blk * BLOCK, BLOCK)
            carry, acc = key_block(k_ref[pl.ds(s0, BLOCK), :], v_ref[pl.ds(s0, BLOCK), :], masks, carry, acc)

        def cond(state):
            j, top, _, _ = state
            return jnp.logical_and(j >= 0, top >= F32_EXP2_UNDERFLOW)

        def body(state):
            j, _, carry, acc = state
            sj = pl.multiple_of(j * BLOCK, BLOCK)
            carry, acc = key_block(k_ref[pl.ds(sj, BLOCK), :], v_ref[pl.ds(sj, BLOCK), :],
                                   tuple(None for _ in subs), carry, acc)
            return j - 1, least_decayed(carry), carry, acc

        _, top, carry, acc = lax.while_loop(cond, body, (loop_from, least_decayed(carry), carry, acc))
        meta_valid = col >= PAD

        def with_meta(carry, acc):
            return key_block(km_ref[...], vm_ref[...], tuple(meta_valid for _ in subs), carry, acc)[1]

        return lax.cond(top >= F32_EXP2_UNDERFLOW, with_meta, lambda carry, acc: acc, carry, acc)

    def first_rows_step():
        sliding_window(True)
        carry = [jnp.zeros((BLOCK, 1), F32) for _ in items]
        acc = [jnp.zeros((SUB, LANES), F32) for _ in items]
        peeled = [(b, tuple(causal(s * SUB - b * BLOCK) for s in subs))
                  for b in reversed(range(ATTN_ROWS // BLOCK))]
        return older_blocks(peeled, jnp.int32(-1), carry, acc)

    def near_window_step():
        visible = col < sub_row + (LOOKBACK - LANES)
        split, log_betas, starts = [], [], []
        for n, (s, p) in enumerate(items):
            start = pl.multiple_of(i * ATTN_ROWS - LOOKBACK + s * SUB, SUB)
            kw = k_ref[pl.ds(start, WIN), p * LANES:(p + 1) * LANES]
            z = lax.dot_general(q_stack[n], kw, (((1,), (1,)), ((), ())), preferred_element_type=F32)
            log_beta, log_1m = _log2_beta_terms(z)
            log_1m = jnp.concatenate([log_1m[:, :LANES], jnp.where(visible, log_1m[:, LANES:], 0.0)], axis=1)
            split.append(log_1m.astype(BF16))
            log_betas.append(log_beta)
            starts.append(start)
        suffix_all = jnp.dot(jnp.concatenate(split, axis=0), later_ref[...], preferred_element_type=F32)
        carry, acc = [], []
        for n, (s, p) in enumerate(items):
            suffix = suffix_all[n * BLOCK:(n + 1) * BLOCK]
            w = jnp.exp2(log_betas[n] + suffix)
            w = jnp.concatenate([w[:, :LANES], jnp.where(visible, w[:, LANES:], 0.0)], axis=1)
            vw = v_ref[pl.ds(starts[n], WIN), p * LANES:(p + 1) * LANES]
            acc.append(unstack(jnp.dot(w.astype(BF16), vw, preferred_element_type=F32)))
            carry.append(suffix[:, WIN - 1:WIN])
        sliding_window(False)

        def keys_before_windows(carry, acc):
            first = i * (ATTN_ROWS // BLOCK)
            peeled, back = [], 1
            while back * BLOCK - LOOKBACK < BLOCK:
                peeled.append((first - back, tuple(prefix(s * SUB - LOOKBACK + back * BLOCK) for s in subs)))
                back += 1
            return older_blocks(peeled, first - back, carry, acc)

        return lax.cond(least_decayed(carry) >= F32_EXP2_UNDERFLOW, keys_before_windows,
                        lambda carry, acc: acc, carry, acc)

    acc = lax.cond(i * ATTN_ROWS >= LOOKBACK, near_window_step, first_rows_step)
    for n, (s, p) in enumerate(items):
        gate = a_ref[s * SUB:(s + 1) * SUB, p * LANES:(p + 1) * LANES].astype(F32)
        o_ref[s * SUB:(s + 1) * SUB, p * LANES:(p + 1) * LANES] = (acc[n] * gate).astype(BF16)


def _attn_call(sinks, sbq, sbk, sbv, sbk_meta, sbv_meta, later, act_sb,
               swq, swkv, swkv_meta, act_sw):
    b, seq, sb_width = sbk.shape
    sw_width, kvw = act_sw.shape[2], swkv.shape[2]
    blocks_per_step = ATTN_ROWS // BLOCK
    rows = lambda width: pl.BlockSpec((None, ATTN_ROWS, width), lambda bi, i: (bi, i, 0))
    whole = pl.BlockSpec((None, seq, sb_width), lambda bi, i: (bi, 0, 0))
    const = lambda shape: pl.BlockSpec(shape, lambda bi, i: (0, 0))
    kv_prev = pl.BlockSpec((None, BLOCK, kvw), lambda bi, i: (bi, jnp.maximum(i * blocks_per_step - 1, 0), 0))
    return pl.pallas_call(
        _attn_kernel,
        grid=(b, seq // ATTN_ROWS),
        in_specs=[pl.BlockSpec(memory_space=pltpu.SMEM),
                  rows(sbq.shape[2]), whole, whole, const((BLOCK, sb_width)), const((BLOCK, sb_width)),
                  const(later.shape), rows(sb_width),
                  rows(swq.shape[2]), kv_prev, rows(kvw), const((BLOCK, kvw)), rows(sw_width)],
        out_specs=[rows(sb_width), rows(sw_width)],
        out_shape=[jax.ShapeDtypeStruct((b, seq, sb_width), BF16), jax.ShapeDtypeStruct((b, seq, sw_width), BF16)],
        compiler_params=pltpu.CompilerParams(dimension_semantics=("arbitrary", "arbitrary"),
                                             vmem_limit_bytes=VMEM_LIMIT_BYTES),
        name="attn",
    )(sinks, sbq, sbk, sbv, sbk_meta, sbv_meta, later, act_sb,
      swq, swkv, swkv, swkv_meta, act_sw)


def _out_kernel(x_ref, asb_ref, asw_ref, ssb_ref, ssw_ref, wsb_ref, wsw_ref, wout_ref, gain_ref, o_ref):
    for r in range(0, x_ref.shape[0], OUT_CHUNK_ROWS):
        rows = slice(r, r + OUT_CHUNK_ROWS)
        y_sb = jnp.dot(asb_ref[rows, :], wsb_ref[...], preferred_element_type=F32)
        y_sw = jnp.dot(asw_ref[rows, :], wsw_ref[...], preferred_element_type=F32)
        merged = ssb_ref[rows, :].astype(F32) * y_sb + ssw_ref[rows, :].astype(F32) * y_sw
        h = x_ref[rows, :] + jnp.dot(merged.astype(BF16), wout_ref[...], preferred_element_type=F32)
        ms = jnp.mean(h * h, axis=-1, keepdims=True)
        o_ref[rows, :] = h * lax.rsqrt(ms + RMS_EPS) * gain_ref[...]


def _out_call(x2d, act_sb, act_sw, sig_sb, sig_sw, w_sb, w_sw, w_out, gain, tile_rows):
    n = x2d.shape[0]
    row_tile = lambda width: pl.BlockSpec((tile_rows, width), lambda i: (i, 0))
    const = lambda shape: pl.BlockSpec(shape, lambda i: (0, 0))
    return pl.pallas_call(
        _out_kernel,
        grid=(n // tile_rows,),
        in_specs=[row_tile(D_MODEL), row_tile(SB_WIDTH), row_tile(SWA_WIDTH), row_tile(D_MODEL),
                  row_tile(D_MODEL), const(w_sb.shape), const(w_sw.shape), const(w_out.shape),
                  const((1, D_MODEL))],
        out_specs=row_tile(D_MODEL),
        out_shape=jax.ShapeDtypeStruct((n, D_MODEL), F32),
        compiler_params=pltpu.CompilerParams(dimension_semantics=("arbitrary",),
                                             vmem_limit_bytes=VMEM_LIMIT_BYTES),
        name="out_proj",
    )(x2d, act_sb, act_sw, sig_sb, sig_sw, w_sb, w_sw, w_out, gain)


def _rope_tables(pos):
    half = HEAD_DIM // 2
    inv = (np.float32(ROPE_THETA) ** (-np.arange(half, dtype=np.float32) / np.float32(half))).astype(np.float32)
    ang = pos.astype(np.float32)[:, None] * inv[None, :]
    cos, sin = np.cos(ang), np.sin(ang)
    reps = LANES // HEAD_DIM
    return (jnp.asarray(np.tile(np.concatenate([cos, cos], axis=1), (1, reps)), F32),
            jnp.asarray(np.tile(np.concatenate([-sin, sin], axis=1), (1, reps)), F32))


def _later_key_selector():
    j = np.arange(WIN)
    later = j[:, None] > j[None, :]
    later[:, WIN - 1] = True
    return jnp.asarray(later, BF16)


def kernel(x, meta_tokens, norm_gain, w_in, w_branch_sb, w_branch_swa, w_out, attn_sinks, final_norm_gain):
    b, seq, d = x.shape
    assert d == D_MODEL and norm_gain.shape[0] == 1 and seq % OUT_TILE_ROWS == 0 and w_in.shape[2] == _C_END
    n = b * seq
    x2d = x.reshape(n, d)
    squeeze = lambda a: a.reshape(a.shape[1:])
    w = squeeze(w_in).astype(BF16)
    gain = norm_gain.reshape(1, d)

    cos, sin_signed = _rope_tables(np.arange(seq) + N_META)
    tok = _proj_call(x2d, gain, w, cos, sin_signed, PROJ_TILE_ROWS)
    sbq, sbk, sbv, swq, swkv, act_sb, act_sw, sig_sb, sig_sw = tok

    meta_tile = jnp.concatenate([jnp.zeros((PAD, d), x.dtype), meta_tokens.astype(x.dtype)], axis=0)
    mcos, msin = _rope_tables(np.maximum(np.arange(BLOCK) - PAD, 0))
    meta = _proj_call(meta_tile, gain, w, mcos, msin, BLOCK)
    sbk_m, sbv_m, swkv_m = meta[1], meta[2], meta[4]

    r3 = lambda a: a.reshape(b, seq, a.shape[1])
    gated_sb, gated_sw = _attn_call(squeeze(attn_sinks).astype(F32), r3(sbq), r3(sbk), r3(sbv), sbk_m, sbv_m,
                                    _later_key_selector(), r3(act_sb),
                                    r3(swq), r3(swkv), swkv_m, r3(act_sw))

    out = _out_call(x2d, gated_sb.reshape(n, SB_WIDTH), gated_sw.reshape(n, SWA_WIDTH), sig_sb, sig_sw,
                    squeeze(w_branch_sb).astype(BF16), squeeze(w_branch_swa).astype(BF16),
                    squeeze(w_out).astype(BF16), final_norm_gain.reshape(1, d), OUT_TILE_ROWS)
    return out.reshape(b, seq, d)
```
